```python
import math, functools
import jax, jax.numpy as jnp
from jax import lax
import numpy as np

D_MODEL = 4096
BATCH = 4
SEQ = 2048
DEPTH = 2
DEC_BATCH = 8
DEC_SEQ = 4
PAST_LEN = 16384
PAGE_SIZE = 128

N_EVEN = (DEPTH + 1) // 2
N_ODD = DEPTH // 2
A_HEADS = D_MODEL // 256
A_DH = 64
A_W = A_HEADS * 2 * A_DH
A_ROT = A_DH // 4
ROPE_THETA = 500000.0
B_HEADS = D_MODEL // 256
B_DK = 128
B_DV = 128
B_QK_W = B_HEADS * B_DK
B_V_W = B_HEADS * B_DV
B_CONV_W = 2 * B_QK_W + B_V_W
CONV_W = 4
DN_CHUNK = 64
C_HEADS = D_MODEL // 128
C_DH = 128
C_W = C_HEADS * C_DH
EVEN_IN_W = 3 * A_W + B_CONV_W + B_V_W + 2 * B_HEADS
EVEN_MIX_W = A_W + B_V_W
ODD_IN_W = 3 * C_W + C_HEADS
N_KEYS = 128
N_EXPERTS = N_KEYS * N_KEYS
PEER_HEADS = 8
PEER_DQ = 256
PEER_TOPK = 16
PEER_BLOCK = 128
Q_BLOCK = 128
EPS = 1e-6

kernel_name = 'hybrid_diffattn_gdn_fox_peer_step'


def rmsnorm(x, g):
    xf = x.astype(jnp.float32)
    y = xf * lax.rsqrt(jnp.mean(xf * xf, axis=-1, keepdims=True) + EPS)
    return (y * g.astype(jnp.float32)).astype(x.dtype)


def l2norm(x):
    return x * lax.rsqrt(jnp.sum(x * x, axis=-1, keepdims=True) + EPS)


def partial_rope(x, pos):
    half = A_ROT // 2
    inv = jnp.power(ROPE_THETA, -jnp.arange(0, A_ROT, 2, dtype=jnp.float32) / A_ROT)
    ang = pos.astype(jnp.float32)[:, None] * inv[None, :]
    cos, sin = jnp.cos(ang)[:, None, :], jnp.sin(ang)[:, None, :]
    xr = x[..., :A_ROT].astype(jnp.float32)
    x1, x2 = xr[..., :half], xr[..., half:]
    rot = jnp.concatenate([x1 * cos - x2 * sin, x2 * cos + x1 * sin], axis=-1).astype(x.dtype)
    return jnp.concatenate([rot, x[..., A_ROT:]], axis=-1)


def diff_core(lam, q, k, v, qpos, kpos):
    q2 = q.reshape(*q.shape[:3], 2, A_DH)
    k2 = k.reshape(*k.shape[:3], 2, A_DH)
    s = jnp.einsum('bqhcd,bkhcd->bhcqk', q2, k2).astype(jnp.float32) * (A_DH ** -0.5)
    s = jnp.where(kpos[None, :] <= qpos[:, None], s, -jnp.inf)
    p = jax.nn.softmax(s, axis=-1)
    p = p[:, :, 0] - lam * p[:, :, 1]
    return jnp.einsum('bhqk,bkhe->bqhe', p.astype(v.dtype), v)


def forget_cumsum(logf):
    logf = logf.astype(jnp.float32)
    return logf - lax.cumsum(logf, axis=1, reverse=True)


def fox_core(q, cq, k, v, ck, qpos, kpos):
    s = jnp.einsum('bqhd,bkhd->bhqk', q, k).astype(jnp.float32) * (C_DH ** -0.5)
    s = s + (jnp.swapaxes(cq, 1, 2)[..., :, None] - jnp.swapaxes(ck, 1, 2)[..., None, :])
    s = jnp.where(kpos[None, :] <= qpos[:, None], s, -jnp.inf)
    p = jax.nn.softmax(s, axis=-1)
    return jnp.einsum('bhqk,bkhd->bqhd', p.astype(v.dtype), v)


def fox_paged_core(q, k, v, logf, qpos, kpos):
    cc = forget_cumsum(logf)
    return fox_core(q, cc[:, -q.shape[1]:], k, v, cc, qpos, kpos)


def blocked_self_attention(core, q_side, k_side):
    seq = q_side[0].shape[1]
    nblk = seq // Q_BLOCK
    kpos = jnp.arange(seq)

    def to_blocks(a):
        return jnp.moveaxis(a.reshape(a.shape[0], nblk, Q_BLOCK, *a.shape[2:]), 1, 0)

    def one(args):
        i, qs = args
        qpos = i * Q_BLOCK + jnp.arange(Q_BLOCK)
        return core(*qs, *k_side, qpos, kpos)

    out = lax.map(one, (jnp.arange(nblk), tuple(to_blocks(a) for a in q_side)))
    out = jnp.moveaxis(out, 0, 1)
    return out.reshape(out.shape[0], seq, *out.shape[3:])


def paged_attention(core, q_side, new_side, pools, layer, page_table):
    past_len = page_table.shape[1] * PAGE_SIZE
    t_new = q_side[0].shape[1]
    qpos = past_len + jnp.arange(t_new)
    kpos = jnp.arange(past_len + t_new)

    def one(args):
        pt_b, qs, ns = args
        ks = tuple(
            jnp.concatenate([pool[layer, pt_b].reshape(past_len, *pool.shape[3:]), n.astype(pool.dtype)], axis=0)[None]
            for pool, n in zip(pools, ns))
        return core(*(a[None] for a in qs), *ks, qpos, kpos)[0]

    return lax.map(one, (page_table, tuple(q_side), tuple(new_side)))


def gated_delta_chunked(q, k, v, g, beta, s0, chunk):
    bsz, t_len, heads, _ = q.shape
    n = t_len // chunk

    def blk(a):
        a = a.reshape(bsz, n, chunk, heads, *a.shape[3:])
        return jnp.moveaxis(jnp.moveaxis(a, 1, 0), 3, 2)

    q, k, v, g, beta = blk(q), blk(k), blk(v), blk(g), blk(beta)
    gc = jnp.cumsum(g, axis=-1)
    causal = jnp.tril(jnp.ones((chunk, chunk), dtype=bool))
    strict = jnp.tril(jnp.ones((chunk, chunk), dtype=bool), -1)
    decay = jnp.exp(jnp.where(causal, gc[..., :, None] - gc[..., None, :], -jnp.inf))
    kb = k * beta[..., None]
    lower = jnp.where(strict, jnp.einsum('nbhid,nbhjd->nbhij', kb, k) * decay, 0.0)
    m = lower + jnp.eye(chunk, dtype=lower.dtype)
    u = lax.linalg.triangular_solve(m, v * beta[..., None], left_side=True, lower=True, unit_diagonal=True)
    w = lax.linalg.triangular_solve(m, kb * jnp.exp(gc)[..., None], left_side=True, lower=True, unit_diagonal=True)
    intra = jnp.where(causal, jnp.einsum('nbhid,nbhjd->nbhij', q, k) * decay, 0.0)

    def step(s, xs):
        q_c, k_c, u_c, w_c, gc_c, a_c = xs
        new_v = u_c - jnp.einsum('bhcd,bhde->bhce', w_c, s)
        o = (jnp.einsum('bhcd,bhde->bhce', q_c * jnp.exp(gc_c)[..., None], s)
             + jnp.einsum('bhij,bhje->bhie', a_c, new_v))
        g_last = gc_c[..., -1]
        s = (s * jnp.exp(g_last)[..., None, None]
             + jnp.einsum('bhcd,bhce->bhde', k_c * jnp.exp(g_last[..., None] - gc_c)[..., None], new_v))
        return s, o

    s_fin, o = lax.scan(step, s0, (q, k, u, w, gc, intra))
    o = jnp.moveaxis(jnp.moveaxis(o, 2, 3), 0, 1)
    return o.reshape(bsz, t_len, heads, -1), s_fin


def gated_deltanet(qkv, z, a, b, conv_buf, s0, conv_w, a_log, dt_bias, norm_g):
    bsz, t_len, _ = qkv.shape
    xp = jnp.concatenate([conv_buf.astype(qkv.dtype), qkv], axis=1)
    conv = sum(xp[:, i:i + t_len] * conv_w[i] for i in range(CONV_W))
    new_buf = xp[:, t_len:]
    act = jax.nn.silu(conv.astype(jnp.float32))
    q = l2norm(act[..., :B_QK_W].reshape(bsz, t_len, B_HEADS, B_DK)) * (B_DK ** -0.5)
    k = l2norm(act[..., B_QK_W:2 * B_QK_W].reshape(bsz, t_len, B_HEADS, B_DK))
    v = act[..., 2 * B_QK_W:].reshape(bsz, t_len, B_HEADS, B_DV)
    g = -jnp.exp(a_log.astype(jnp.float32)) * jax.nn.softplus(a.astype(jnp.float32) + dt_bias.astype(jnp.float32))
    beta = jax.nn.sigmoid(b.astype(jnp.float32))
    chunk = DN_CHUNK if t_len % DN_CHUNK == 0 else t_len
    o, s_new = gated_delta_chunked(q, k, v, g, beta, s0.astype(jnp.float32), chunk)
    o = rmsnorm(o, norm_g) * jax.nn.silu(z.astype(jnp.float32)).reshape(bsz, t_len, B_HEADS, B_DV)
    return o.reshape(bsz, t_len, B_V_W).astype(qkv.dtype), new_buf, s_new.astype(s0.dtype)


def even_projection(xn, pos, w_in):
    bsz, t_len, _ = xn.shape
    proj = xn @ w_in
    o1, o2, o3 = A_W, 2 * A_W, 3 * A_W
    o4 = o3 + B_CONV_W
    o5 = o4 + B_V_W
    o6 = o5 + B_HEADS

    def rope_heads(t):
        return partial_rope(t.reshape(bsz, t_len, 2 * A_HEADS, A_DH), pos).reshape(bsz, t_len, A_HEADS, 2 * A_DH)

    qa = rope_heads(proj[..., :o1])
    ka = rope_heads(proj[..., o1:o2])
    va = proj[..., o2:o3].reshape(bsz, t_len, A_HEADS, 2 * A_DH)
    return qa, ka, va, (proj[..., o3:o4], proj[..., o4:o5], proj[..., o5:o6], proj[..., o6:])


def even_layer(h, pos, ln_g, attend, conv_buf, rec0, lam, lam_init,
               w_in, w_out, subln_g, conv_w, a_log, dt_bias, dn_norm_g):
    qa, ka, va, (qkv_b, z, a, b) = even_projection(rmsnorm(h, ln_g), pos, w_in)
    ctx = attend(functools.partial(diff_core, lam), qa, ka, va)
    dn, conv_new, rec_new = gated_deltanet(qkv_b, z, a, b, conv_buf, rec0, conv_w, a_log, dt_bias, dn_norm_g)
    att = rmsnorm(ctx, subln_g) * (1.0 - lam_init)
    mix = jnp.concatenate([att.reshape(*att.shape[:2], A_W), dn], axis=-1)
    return h + mix @ w_out, ka, va, conv_new, rec_new


def odd_layer(h, ln_g, attend, w_in, b_f, w_out):
    bsz, t_len, _ = h.shape
    proj = rmsnorm(h, ln_g) @ w_in
    q = proj[..., :C_W].reshape(bsz, t_len, C_HEADS, C_DH)
    k = proj[..., C_W:2 * C_W].reshape(bsz, t_len, C_HEADS, C_DH)
    v = proj[..., 2 * C_W:3 * C_W].reshape(bsz, t_len, C_HEADS, C_DH)
    logf = jax.nn.log_sigmoid(proj[..., 3 * C_W:].astype(jnp.float32) + b_f.astype(jnp.float32))
    ctx = attend(q, k, v, logf)
    return h + ctx.reshape(bsz, t_len, C_W) @ w_out, k, v, logf


def prompt_fox(q, k, v, logf):
    cc = forget_cumsum(logf)
    return blocked_self_attention(fox_core, (q, cc), (k, v, cc))


def peer(xn, w_q, sub_keys, u_tab, v_tab):
    bsz, t_len, d = xn.shape
    n = bsz * t_len
    blk = math.gcd(n, PEER_BLOCK)
    xb = xn.reshape(n // blk, blk, d)
    kk = PEER_TOPK * PEER_TOPK

    def one(xc):
        q = (xc @ w_q).reshape(blk, PEER_HEADS, 2, PEER_DQ // 2)
        s = jnp.einsum('nhcd,ckd->nhck', q, sub_keys).astype(jnp.float32)
        s1, i1 = lax.top_k(s[:, :, 0], PEER_TOPK)
        s2, i2 = lax.top_k(s[:, :, 1], PEER_TOPK)
        cand_s = (s1[..., :, None] + s2[..., None, :]).reshape(blk, PEER_HEADS, kk)
        cand_e = (i1[..., :, None] * N_KEYS + i2[..., None, :]).reshape(blk, PEER_HEADS, kk)
        top_s, top_j = lax.top_k(cand_s, PEER_TOPK)
        eid = jnp.take_along_axis(cand_e, top_j, axis=-1).reshape(blk, PEER_HEADS * PEER_TOPK)
        gate = jax.nn.softmax(top_s, axis=-1).reshape(blk, PEER_HEADS * PEER_TOPK)
        act = jax.nn.gelu(jnp.einsum('nkd,nd->nk', u_tab[eid], xc).astype(jnp.float32), approximate=False) * gate
        return jnp.einsum('nk,nkd->nd', act.astype(xc.dtype), v_tab[eid])

    return lax.map(one, xb).reshape(bsz, t_len, d)


def setup_inputs(seed: int = 0) -> dict:
    key = jax.random.key(seed)
    keys = jax.random.split(key, 48)
    counter = iter(range(48))
    f32 = jnp.float32

    def nk():
        return keys[next(counter)]

    def normal(shape, scale):
        return jax.random.normal(nk(), shape, f32) * scale

    def uniform(shape, lo, hi):
        return jax.random.uniform(nk(), shape, f32, lo, hi)

    def gain(shape):
        return 1.0 + 0.02 * jax.random.normal(nk(), shape, f32)

    n_pages = PAST_LEN // PAGE_SIZE
    n_pool = (5 * DEC_BATCH * n_pages + 3) // 4
    x_prompt = normal((BATCH, SEQ, D_MODEL), 1.0)
    x_sample = normal((DEC_BATCH, DEC_SEQ, D_MODEL), 1.0)
    cache_a_k = normal((N_EVEN, n_pool, PAGE_SIZE, A_HEADS, 2 * A_DH), 1.0)
    cache_a_v = normal((N_EVEN, n_pool, PAGE_SIZE, A_HEADS, 2 * A_DH), 1.0)
    state_b_conv = normal((N_EVEN, DEC_BATCH, CONV_W - 1, B_CONV_W), 1.0)
    state_b_rec = normal((N_EVEN, DEC_BATCH, B_HEADS, B_DK, B_DV), B_DK ** -0.5)
    cache_c_k = normal((N_ODD, n_pool, PAGE_SIZE, C_HEADS, C_DH), 1.0)
    cache_c_v = normal((N_ODD, n_pool, PAGE_SIZE, C_HEADS, C_DH), 1.0)
    lf_shape = (N_ODD, n_pool, PAGE_SIZE, C_HEADS)
    cache_c_logf = jax.nn.log_sigmoid(normal(lf_shape, 1.0) + uniform(lf_shape, 2.0, 6.0))
    perm = jax.random.permutation(nk(), n_pool)[:DEC_BATCH * n_pages]
    page_table = perm.reshape(DEC_BATCH, n_pages).astype(jnp.int32)
    dt = jnp.exp(uniform((N_EVEN, B_HEADS), math.log(1e-3), math.log(1e-1)))
    return {
        'x_prompt': x_prompt,
        'x_sample': x_sample,
        'cache_a_k': cache_a_k,
        'cache_a_v': cache_a_v,
        'state_b_conv': state_b_conv,
        'state_b_rec': state_b_rec,
        'cache_c_k': cache_c_k,
        'cache_c_v': cache_c_v,
        'cache_c_logf': cache_c_logf,
        'page_table': page_table,
        'ln_mix': gain((DEPTH, D_MODEL)),
        'ln_ffn': gain((DEPTH, D_MODEL)),
        'ln_out': gain((D_MODEL,)),
        'w_in_even': normal((N_EVEN, D_MODEL, EVEN_IN_W), D_MODEL ** -0.5),
        'w_out_even': normal((N_EVEN, EVEN_MIX_W, D_MODEL), EVEN_MIX_W ** -0.5),
        'lam_q1': normal((N_EVEN, A_DH), 0.1),
        'lam_k1': normal((N_EVEN, A_DH), 0.1),
        'lam_q2': normal((N_EVEN, A_DH), 0.1),
        'lam_k2': normal((N_EVEN, A_DH), 0.1),
        'subln_g': gain((N_EVEN, 2 * A_DH)),
        'conv_w': normal((N_EVEN, CONV_W, B_CONV_W), CONV_W ** -0.5),
        'a_log': jnp.log(uniform((N_EVEN, B_HEADS), 1.0, 16.0)),
        'dt_bias': jnp.log(jnp.expm1(dt)),
        'dn_norm_g': gain((N_EVEN, B_DV)),
        'w_in_odd': normal((N_ODD, D_MODEL, ODD_IN_W), D_MODEL ** -0.5),
        'b_forget': uniform((N_ODD, C_HEADS), 2.0, 6.0),
        'w_out_odd': normal((N_ODD, C_W, D_MODEL), C_W ** -0.5),
        'peer_wq': normal((DEPTH, D_MODEL, PEER_HEADS * PEER_DQ), D_MODEL ** -0.5),
        'peer_keys': normal((DEPTH, 2, N_KEYS, PEER_DQ // 2), (PEER_DQ // 2) ** -0.5),
        'peer_u': normal((DEPTH, N_EXPERTS, D_MODEL), D_MODEL ** -0.5),
        'peer_v': normal((DEPTH, N_EXPERTS, D_MODEL), PEER_HEADS ** -0.5),
    }


def reference(x_prompt, x_sample, cache_a_k, cache_a_v, state_b_conv, state_b_rec,
              cache_c_k, cache_c_v, cache_c_logf, page_table,
              ln_mix, ln_ffn, ln_out, w_in_even, w_out_even, lam_q1, lam_k1, lam_q2, lam_k2,
              subln_g, conv_w, a_log, dt_bias, dn_norm_g, w_in_odd, b_forget, w_out_odd,
              peer_wq, peer_keys, peer_u, peer_v):
    past_len = page_table.shape[1] * PAGE_SIZE
    bsz, seq, _ = x_prompt.shape
    dbsz, dseq, _ = x_sample.shape
    pos_p = jnp.arange(seq)
    pos_s = past_len + jnp.arange(dseq)
    hp, hs = x_prompt, x_sample
    ak_p, av_p, bc_p, br_p, ck_p, cv_p, cf_p = [], [], [], [], [], [], []
    ak_s, av_s, bc_s, br_s, ck_s, cv_s, cf_s = [], [], [], [], [], [], []
    for li in range(DEPTH):
        if li % 2 == 0:
            e = li // 2
            lam_init = 0.8 - 0.6 * math.exp(-0.3 * li)
            lam = (jnp.exp(jnp.sum(lam_q1[e] * lam_k1[e]).astype(jnp.float32))
                   - jnp.exp(jnp.sum(lam_q2[e] * lam_k2[e]).astype(jnp.float32)) + lam_init)
            wts = (w_in_even[e], w_out_even[e], subln_g[e], conv_w[e], a_log[e], dt_bias[e], dn_norm_g[e])
            attend_p = lambda core, q, k, v: blocked_self_attention(core, (q,), (k, v))
            hp, ka, va, cb, rs = even_layer(
                hp, pos_p, ln_mix[li], attend_p,
                jnp.zeros((bsz, CONV_W - 1, B_CONV_W), hp.dtype),
                jnp.zeros((bsz, B_HEADS, B_DK, B_DV), jnp.float32), lam, lam_init, *wts)
            ak_p.append(ka); av_p.append(va); bc_p.append(cb); br_p.append(rs)
            attend_s = lambda core, q, k, v: paged_attention(core, (q,), (k, v), (cache_a_k, cache_a_v), e, page_table)
            hs, ka, va, cb, rs = even_layer(
                hs, pos_s, ln_mix[li], attend_s, state_b_conv[e], state_b_rec[e], lam, lam_init, *wts)
            ak_s.append(ka); av_s.append(va); bc_s.append(cb); br_s.append(rs)
        else:
            o = li // 2
            hp, kc, vc, lf = odd_layer(hp, ln_mix[li], prompt_fox, w_in_odd[o], b_forget[o], w_out_odd[o])
            ck_p.append(kc); cv_p.append(vc); cf_p.append(lf)
            attend_s = lambda q, k, v, lf_new: paged_attention(
                fox_paged_core, (q,), (k, v, lf_new), (cache_c_k, cache_c_v, cache_c_logf), o, page_table)
            hs, kc, vc, lf = odd_layer(hs, ln_mix[li], attend_s, w_in_odd[o], b_forget[o], w_out_odd[o])
            ck_s.append(kc); cv_s.append(vc); cf_s.append(lf)
        u_tab, v_tab = peer_u[li], peer_v[li]
        hp = hp + peer(rmsnorm(hp, ln_ffn[li]), peer_wq[li], peer_keys[li], u_tab, v_tab)
        hs = hs + peer(rmsnorm(hs, ln_ffn[li]), peer_wq[li], peer_keys[li], u_tab, v_tab)
    y_prompt = rmsnorm(hp, ln_out)
    y_sample = rmsnorm(hs, ln_out)
    new_a_k_prompt, new_a_v_prompt = jnp.stack(ak_p), jnp.stack(av_p)
    new_b_conv_prompt, new_b_rec_prompt = jnp.stack(bc_p), jnp.stack(br_p)
    new_c_k_prompt, new_c_v_prompt, new_c_logf_prompt = jnp.stack(ck_p), jnp.stack(cv_p), jnp.stack(cf_p)
    new_a_k_sample, new_a_v_sample = jnp.stack(ak_s), jnp.stack(av_s)
    new_b_conv_sample, new_b_rec_sample = jnp.stack(bc_s), jnp.stack(br_s)
    new_c_k_sample, new_c_v_sample, new_c_logf_sample = jnp.stack(ck_s), jnp.stack(cv_s), jnp.stack(cf_s)
    return (y_prompt, y_sample,
            new_a_k_prompt, new_a_v_prompt, new_b_conv_prompt, new_b_rec_prompt,
            new_c_k_prompt, new_c_v_prompt, new_c_logf_prompt,
            new_a_k_sample, new_a_v_sample, new_b_conv_sample, new_b_rec_sample,
            new_c_k_sample, new_c_v_sample, new_c_logf_sample)
```

```python
import functools
import math

import jax
import jax.numpy as jnp
from jax import lax
from jax.experimental import pallas as pl
from jax.experimental.pallas import tpu as pltpu

F32 = jnp.float32
BF16 = jnp.bfloat16
HI = lax.Precision.HIGHEST

EPS = 1e-6
A_DH = 64
ROPE_THETA = 500000.0
CONV_TAPS = 4
DN_CHUNK = 64
PEER_TOPK = 16
PAGE = 128
LANES = 128
SUBLANES = 8
VMEM_LIMIT = 56 * 1024 * 1024

NT = (((1,), (1,)), ((), ()))
TN = (((0,), (0,)), ((), ()))


def _cparams(*sem):
    return pltpu.CompilerParams(dimension_semantics=sem, vmem_limit_bytes=VMEM_LIMIT)


def _tile(n, pref):
    if n <= pref:
        return n
    t = pref
    while n % t:
        t //= 2
    return t


def _rms_kernel(*refs, n_in, emit_sum, transpose_out):
    x = refs[0][...]
    for r in refs[1:n_in]:
        x = x + r[...]
    g = refs[n_in][...]
    outs = refs[n_in + 1:]
    y = x * lax.rsqrt(jnp.mean(x * x, axis=-1, keepdims=True) + EPS) * g
    k = 0
    if emit_sum:
        outs[0][...] = x
        k = 1
    if transpose_out:
        outs[k][...] = y.T.astype(outs[k].dtype)
    else:
        outs[k][...] = y.astype(outs[k].dtype)


def rms_norm(xs, g, *, out_dtype=BF16, emit_sum=False, transpose_out=False):
    m, d = xs[0].shape
    tm = _tile(m, 256)
    row = pl.BlockSpec((tm, d), lambda i: (i, 0))
    out_shape, out_specs = [], []
    if emit_sum:
        out_shape.append(jax.ShapeDtypeStruct((m, d), F32))
        out_specs.append(row)
    if transpose_out:
        out_shape.append(jax.ShapeDtypeStruct((d, m), out_dtype))
        out_specs.append(pl.BlockSpec((d, tm), lambda i: (0, i)))
    else:
        out_shape.append(jax.ShapeDtypeStruct((m, d), out_dtype))
        out_specs.append(row)
    res = pl.pallas_call(
        functools.partial(_rms_kernel, n_in=len(xs), emit_sum=emit_sum, transpose_out=transpose_out),
        out_shape=out_shape,
        grid=(m // tm,),
        in_specs=[row] * len(xs) + [pl.BlockSpec((1, d), lambda i: (0, 0))],
        out_specs=out_specs,
        compiler_params=_cparams("parallel"),
        name="rms_norm",
    )(*xs, g.reshape(1, d).astype(F32))
    return res if emit_sum else res[0]


def _mm_kernel(*refs, mode, nk):
    a_ref, b_ref = refs[0], refs[1]
    o_ref = refs[-1]
    acc = jnp.dot(a_ref[...], b_ref[...], preferred_element_type=F32)
    if mode == "plain":
        o_ref[...] = acc
    elif mode == "resid":
        o_ref[...] = acc + refs[2][...]
    elif mode == "rope":
        c, sp, sm = refs[2][...], refs[3][...], refs[4][...]
        for g in range(acc.shape[1] // LANES):
            x = acc[:, g * LANES:(g + 1) * LANES]
            o_ref[:, g * LANES:(g + 1) * LANES] = (
                x * c + pltpu.roll(x, LANES - 8, axis=1) * sp + pltpu.roll(x, 8, axis=1) * sm)
    elif mode == "logsig":
        x = acc + refs[2][...]
        o_ref[...] = jnp.minimum(x, 0.0) - jnp.log1p(jnp.exp(-jnp.abs(x)))
    elif mode == "keys":
        keys_ref = refs[2]
        for c in range(2):
            q = acc[c * nk:(c + 1) * nk, :].astype(BF16)
            o_ref[c * nk:(c + 1) * nk, :] = jnp.dot(keys_ref[c], q, preferred_element_type=F32)
    else:
        raise ValueError(mode)


def matmul(a, b, *, mode="plain", extra=(), tm=512, tn=512, nk=0):
    m, k = a.shape
    _, n = b.shape
    tm = _tile(m, tm)
    tn = _tile(n, tn)
    in_specs = [pl.BlockSpec((tm, k), lambda i, j: (i, 0)),
                pl.BlockSpec((k, tn), lambda i, j: (0, j))]
    if mode == "resid":
        in_specs.append(pl.BlockSpec((tm, tn), lambda i, j: (i, j)))
    elif mode == "rope":
        in_specs += [pl.BlockSpec((tm, LANES), lambda i, j: (i, 0))] * 3
    elif mode == "logsig":
        in_specs.append(pl.BlockSpec((1, tn), lambda i, j: (0, j)))
    elif mode == "keys":
        in_specs.append(pl.BlockSpec(extra[0].shape, lambda i, j: (0, 0, 0)))
    return pl.pallas_call(
        functools.partial(_mm_kernel, mode=mode, nk=nk),
        out_shape=jax.ShapeDtypeStruct((m, n), F32),
        grid=(m // tm, n // tn),
        in_specs=in_specs,
        out_specs=pl.BlockSpec((tm, tn), lambda i, j: (i, j)),
        compiler_params=_cparams("parallel", "parallel"),
        name="matmul_" + mode,
    )(a, b, *extra)


def _flash_init(m_s, l_s, acc_s):
    m_s[...] = jnp.full(m_s.shape, -jnp.inf, F32)
    l_s[...] = jnp.zeros(l_s.shape, F32)
    acc_s[...] = jnp.zeros(acc_s.shape, F32)


def _flash_update(c, s, v, m_s, l_s, acc_s):
    m_prev = m_s[c]
    m_new = jnp.maximum(m_prev, jnp.max(s, axis=1, keepdims=True))
    alpha = jnp.exp(m_prev - m_new)
    p = jnp.exp(s - m_new)
    l_s[c] = alpha * l_s[c] + jnp.sum(p, axis=1, keepdims=True)
    acc_s[c] = alpha * acc_s[c] + jnp.dot(p.astype(BF16), v, preferred_element_type=F32)
    m_s[c] = m_new


def _diff_flash_kernel(lam_ref, q_ref, k_ref, v_ref, g_ref, o_ref, m_s, l_s, acc_s, *, tq, tk, post):
    i, j = pl.program_id(2), pl.program_id(3)

    @pl.when(j == 0)
    def _():
        _flash_init(m_s, l_s, acc_s)

    @pl.when(j * tk <= i * tq + (tq - 1))
    def _():
        q = q_ref[...].astype(BF16)
        k = k_ref[...].astype(BF16)
        v = v_ref[...].astype(BF16)
        row = i * tq + lax.broadcasted_iota(jnp.int32, (tq, tk), 0)
        col = j * tk + lax.broadcasted_iota(jnp.int32, (tq, tk), 1)
        keep = col <= row
        for c in range(2):
            s = lax.dot_general(q[:, c * A_DH:(c + 1) * A_DH], k[:, c * A_DH:(c + 1) * A_DH], NT,
                                preferred_element_type=F32) * (A_DH ** -0.5)
            _flash_update(c, jnp.where(keep, s, -jnp.inf), v, m_s, l_s, acc_s)

    @pl.when(j == pl.num_programs(3) - 1)
    def _():
        o = acc_s[0] / l_s[0] - lam_ref[0] * (acc_s[1] / l_s[1])
        y = o * lax.rsqrt(jnp.mean(o * o, axis=-1, keepdims=True) + EPS) * g_ref[...]
        o_ref[...] = (y * post).astype(o_ref.dtype)


def diff_attention_prompt(qa, ka, va, lam, subln_g, post, *, bsz, seq, heads):
    dv = 2 * A_DH
    tq = tk = _tile(seq, 512)
    nq, nk = seq // tq, seq // tk
    kv_map = lambda b, h, i, j: (b * nk + jnp.minimum(j, (i * tq + tq - 1) // tk), h)
    return pl.pallas_call(
        functools.partial(_diff_flash_kernel, tq=tq, tk=tk, post=post),
        out_shape=jax.ShapeDtypeStruct((bsz * seq, heads * dv), BF16),
        grid=(bsz, heads, nq, nk),
        in_specs=[pl.BlockSpec(memory_space=pltpu.SMEM),
                  pl.BlockSpec((tq, dv), lambda b, h, i, j: (b * nq + i, h)),
                  pl.BlockSpec((tk, dv), kv_map),
                  pl.BlockSpec((tk, dv), kv_map),
                  pl.BlockSpec((1, dv), lambda b, h, i, j: (0, 0))],
        out_specs=pl.BlockSpec((tq, dv), lambda b, h, i, j: (b * nq + i, h)),
        scratch_shapes=[pltpu.VMEM((2, tq, 1), F32), pltpu.VMEM((2, tq, 1), F32),
                        pltpu.VMEM((2, tq, dv), F32)],
        compiler_params=_cparams("parallel", "parallel", "parallel", "arbitrary"),
        name="diff_attention_prompt",
    )(lam.reshape(1).astype(F32), qa, ka, va, subln_g.reshape(1, dv).astype(F32))


def _fox_flash_kernel(q_ref, k_ref, v_ref, fq_ref, fk_ref, o_ref, m_s, l_s, acc_s, *, tq, tk, scale):
    i, j = pl.program_id(2), pl.program_id(3)

    @pl.when(j == 0)
    def _():
        _flash_init(m_s, l_s, acc_s)

    @pl.when(j * tk <= i * tq + (tq - 1))
    def _():
        q = q_ref[...].astype(BF16)
        k = k_ref[...].astype(BF16)
        v = v_ref[...].astype(BF16)
        row = i * tq + lax.broadcasted_iota(jnp.int32, (tq, tk), 0)
        col = j * tk + lax.broadcasted_iota(jnp.int32, (tq, tk), 1)
        s = lax.dot_general(q, k, NT, preferred_element_type=F32) * scale + (fq_ref[...] - fk_ref[...])
        _flash_update(0, jnp.where(col <= row, s, -jnp.inf), v, m_s, l_s, acc_s)

    @pl.when(j == pl.num_programs(3) - 1)
    def _():
        o_ref[...] = (acc_s[0] / l_s[0]).astype(o_ref.dtype)


def fox_attention_prompt(q, k, v, f_col, f_row, *, bsz, seq, heads, dh):
    tq = tk = _tile(seq, 512)
    nq, nk = seq // tq, seq // tk
    last = lambda i, j: jnp.minimum(j, (i * tq + tq - 1) // tk)
    kv_map = lambda b, h, i, j: (b * nk + last(i, j), h)
    return pl.pallas_call(
        functools.partial(_fox_flash_kernel, tq=tq, tk=tk, scale=dh ** -0.5),
        out_shape=jax.ShapeDtypeStruct((bsz * seq, heads * dh), BF16),
        grid=(bsz, heads, nq, nk),
        in_specs=[pl.BlockSpec((tq, dh), lambda b, h, i, j: (b * nq + i, h)),
                  pl.BlockSpec((tk, dh), kv_map),
                  pl.BlockSpec((tk, dh), kv_map),
                  pl.BlockSpec((None, None, tq, 1), lambda b, h, i, j: (b, h, i, 0)),
                  pl.BlockSpec((None, None, 1, tk), lambda b, h, i, j: (b, h, 0, last(i, j)))],
        out_specs=pl.BlockSpec((tq, dh), lambda b, h, i, j: (b * nq + i, h)),
        scratch_shapes=[pltpu.VMEM((1, tq, 1), F32), pltpu.VMEM((1, tq, 1), F32),
                        pltpu.VMEM((1, tq, dh), F32)],
        compiler_params=_cparams("parallel", "parallel", "parallel", "arbitrary"),
        name="fox_attention_prompt",
    )(q, k, v, f_col, f_row)


def _cumsum_kernel(x_ref, o_ref, carry_s, *, tc):
    @pl.when(pl.program_id(1) == 0)
    def _():
        carry_s[...] = jnp.zeros(carry_s.shape, F32)

    r = lax.broadcasted_iota(jnp.int32, (tc, tc), 0)
    c = lax.broadcasted_iota(jnp.int32, (tc, tc), 1)
    tri = (c <= r).astype(F32)
    y = jnp.dot(tri, x_ref[...], precision=HI, preferred_element_type=F32) + carry_s[...]
    o_ref[...] = y
    carry_s[...] = y[tc - 1:tc, :]


def cumsum_time(x):
    bsz, seq, w = x.shape
    tc = _tile(seq, 256)
    return pl.pallas_call(
        functools.partial(_cumsum_kernel, tc=tc),
        out_shape=jax.ShapeDtypeStruct(x.shape, F32),
        grid=(bsz, seq // tc),
        in_specs=[pl.BlockSpec((None, tc, w), lambda b, i: (b, i, 0))],
        out_specs=pl.BlockSpec((None, tc, w), lambda b, i: (b, i, 0)),
        scratch_shapes=[pltpu.VMEM((1, w), F32)],
        compiler_params=_cparams("parallel", "arbitrary"),
        name="cumsum_time",
    )(x)


def _unit_lower_inverse(n, size):
    dot = lambda a, b: jnp.dot(a, b, precision=HI, preferred_element_type=F32)
    ri = lax.broadcasted_iota(jnp.int32, (size, size), 0)
    ci = lax.broadcasted_iota(jnp.int32, (size, size), 1)
    eye = (ri == ci).astype(F32)
    bs = min(size, 16)
    nd = jnp.where((ri // bs) == (ci // bs), n, 0.0)
    x, pw, p = eye - nd, dot(nd, nd), 2
    while p < bs:
        x = dot(x, eye + pw)
        if 2 * p < bs:
            pw = dot(pw, pw)
        p *= 2
    if size == bs:
        return x
    nb = size // bs
    blk = dot(x, n - nd)
    y = eye - blk
    if nb > 2:
        pw, p = dot(blk, blk), 2
        while p < nb:
            y = dot(y, eye + pw)
            if 2 * p < nb:
                pw = dot(pw, pw)
            p *= 2
    return dot(y, x)


def _gdn_kernel(nea_ref, dtb_ref, xq_ref, xk_ref, xv_ref, z_ref, a_ref, b_ref,
                cq_ref, ck_ref, cv_ref, wq_ref, wk_ref, wv_ref, s0_ref, ng_ref,
                o_ref, sfin_ref, prev_s, state_s, *, chunk, valid, dk):
    h, c = pl.program_id(1), pl.program_id(2)
    halo = SUBLANES
    dot = lambda a, b: jnp.dot(a, b, precision=HI, preferred_element_type=F32)

    @pl.when(c == 0)
    def _():
        prev_s[0] = cq_ref[...]
        prev_s[1] = ck_ref[...]
        prev_s[2] = cv_ref[...]
        state_s[...] = s0_ref[...]

    def conv_act(idx, x_ref, w_ref):
        x = x_ref[...]
        w = w_ref[...]
        xx = jnp.concatenate([prev_s[idx], x], axis=0)
        y = x * w[CONV_TAPS - 1:CONV_TAPS, :]
        for s in range(1, CONV_TAPS):
            y = y + pltpu.roll(xx, s, axis=0)[halo:halo + chunk] * w[CONV_TAPS - 1 - s:CONV_TAPS - s, :]
        prev_s[idx] = x[chunk - halo:chunk]
        return y / (1.0 + jnp.exp(-y))

    aq, ak, v = conv_act(0, xq_ref, wq_ref), conv_act(1, xk_ref, wk_ref), conv_act(2, xv_ref, wv_ref)
    q = aq * lax.rsqrt(jnp.sum(aq * aq, axis=-1, keepdims=True) + EPS) * (dk ** -0.5)
    k = ak * lax.rsqrt(jnp.sum(ak * ak, axis=-1, keepdims=True) + EPS)
    x = a_ref[...] + dtb_ref[h]
    g = nea_ref[h] * (jnp.maximum(x, 0.0) + jnp.log1p(jnp.exp(-jnp.abs(x))))
    beta = 1.0 / (1.0 + jnp.exp(-b_ref[...]))
    if valid < chunk:
        live = lax.broadcasted_iota(jnp.int32, (chunk, 1), 0) < valid
        g = jnp.where(live, g, 0.0)
        beta = jnp.where(live, beta, 0.0)
        k = jnp.where(live, k, 0.0)

    ri = lax.broadcasted_iota(jnp.int32, (chunk, chunk), 0)
    ci = lax.broadcasted_iota(jnp.int32, (chunk, chunk), 1)
    incl = ci <= ri
    tri = incl.astype(F32)
    gc_b = dot(tri, jnp.broadcast_to(g, (chunk, dk)))
    gc_row = dot(jnp.ones((chunk, chunk), F32),
                 jnp.where(ri <= ci, jnp.broadcast_to(g, (chunk, chunk)), 0.0))
    decay = jnp.exp(jnp.where(incl, gc_b[:, :chunk] - gc_row, -jnp.inf))
    kb = k * beta
    lower = jnp.where(ci < ri, lax.dot_general(kb, k, NT, precision=HI, preferred_element_type=F32) * decay, 0.0)
    intra = jnp.where(incl, lax.dot_general(q, k, NT, precision=HI, preferred_element_type=F32) * decay, 0.0)
    inv = _unit_lower_inverse(lower, chunk)
    egc = jnp.exp(gc_b)
    u = dot(inv, v * beta)
    w = dot(inv, kb * egc)
    state = state_s[...]
    new_v = u - dot(w, state)
    o = dot(q * egc, state) + dot(intra, new_v)
    g_last = gc_b[chunk - 1:chunk, :]
    state = state * jnp.exp(g_last) + lax.dot_general(k * jnp.exp(g_last - gc_b), new_v, TN,
                                                      precision=HI, preferred_element_type=F32)
    state_s[...] = state

    zz = z_ref[...]
    y = o * lax.rsqrt(jnp.mean(o * o, axis=-1, keepdims=True) + EPS) * ng_ref[...]
    o_ref[...] = (y * (zz / (1.0 + jnp.exp(-zz)))).astype(o_ref.dtype)

    @pl.when(c == pl.num_programs(2) - 1)
    def _():
        sfin_ref[...] = state


def gated_deltanet(qkv, z, a, b, conv_buf8, s0, conv_w, a_log, dt_bias, norm_g, *, bsz, seq, heads, valid):
    dk = LANES
    chunk = DN_CHUNK
    nch = seq // chunk
    assert nch == 1 or valid == chunk
    col = lambda off: (lambda bb, h, c: (bb * nch + c, off + h))
    buf = lambda off: (lambda bb, h, c: (bb, 0, off + h))
    wsp = lambda off: (lambda bb, h, c: (0, off + h))
    gate = pl.BlockSpec((None, None, chunk, 1), lambda bb, h, c: (bb, h, c, 0))
    st = pl.BlockSpec((None, None, dk, dk), lambda bb, h, c: (bb, h, 0, 0))
    smem = pl.BlockSpec(memory_space=pltpu.SMEM)
    return pl.pallas_call(
        functools.partial(_gdn_kernel, chunk=chunk, valid=valid, dk=dk),
        out_shape=[jax.ShapeDtypeStruct((bsz * seq, heads * dk), BF16),
                   jax.ShapeDtypeStruct((bsz, heads, dk, dk), F32)],
        grid=(bsz, heads, nch),
        in_specs=[smem, smem,
                  pl.BlockSpec((chunk, dk), col(0)), pl.BlockSpec((chunk, dk), col(heads)),
                  pl.BlockSpec((chunk, dk), col(2 * heads)), pl.BlockSpec((chunk, dk), col(0)),
                  gate, gate,
                  pl.BlockSpec((None, SUBLANES, dk), buf(0)), pl.BlockSpec((None, SUBLANES, dk), buf(heads)),
                  pl.BlockSpec((None, SUBLANES, dk), buf(2 * heads)),
                  pl.BlockSpec((CONV_TAPS, dk), wsp(0)), pl.BlockSpec((CONV_TAPS, dk), wsp(heads)),
                  pl.BlockSpec((CONV_TAPS, dk), wsp(2 * heads)),
                  st, pl.BlockSpec((1, dk), lambda bb, h, c: (0, 0))],
        out_specs=[pl.BlockSpec((chunk, dk), col(0)), st],
        scratch_shapes=[pltpu.VMEM((3, SUBLANES, dk), F32), pltpu.VMEM((dk, dk), F32)],
        compiler_params=_cparams("parallel", "parallel", "arbitrary"),
        name="gated_deltanet",
    )(-jnp.exp(a_log.astype(F32)), dt_bias.astype(F32), qkv, qkv, qkv, z, a, b,
      conv_buf8, conv_buf8, conv_buf8, conv_w, conv_w, conv_w, s0, norm_g.reshape(1, dk).astype(F32))


def _extract_top(s, count):
    rows = lax.broadcasted_iota(jnp.int32, s.shape, 0)
    work = s
    member = jnp.zeros(s.shape, F32)
    vals = []
    for _ in range(count):
        m = jnp.max(work, axis=0, keepdims=True)
        idx = jnp.min(jnp.where(work == m, rows, s.shape[0]), axis=0, keepdims=True)
        hit = rows == idx
        member = jnp.where(hit, 1.0, member)
        work = jnp.where(hit, -jnp.inf, work)
        vals.append(m)
    return vals, member > 0.5


def _peer_select_kernel(s_ref, a1_ref, b2_ref, tau_ref, *, nk):
    s1 = s_ref[0:nk, :]
    s2 = s_ref[nk:2 * nk, :]
    tl = s1.shape[1]
    v1, mem1 = _extract_top(s1, PEER_TOPK)
    v2, mem2 = _extract_top(s2, PEER_TOPK)
    sub = lax.broadcasted_iota(jnp.int32, (PEER_TOPK, tl), 0)
    v2_all = jnp.zeros((PEER_TOPK, tl), F32)
    for t in range(PEER_TOPK):
        v2_all = jnp.where(sub == t, v2[t], v2_all)
    cand = jnp.concatenate([v1[t] + v2_all for t in range(PEER_TOPK)], axis=0)
    top, _ = _extract_top(cand, PEER_TOPK)
    z = jnp.ones((1, tl), F32)
    for t in range(1, PEER_TOPK):
        z = z + jnp.exp(top[t] - top[0])
    a1_ref[...] = jnp.where(mem1, jnp.exp(s1 - v1[0]), 0.0) / z
    b2_ref[...] = jnp.where(mem2, jnp.exp(s2 - v2[0]), 0.0)
    tau_ref[...] = top[PEER_TOPK - 1]


def peer_select(s_t, *, heads, nk):
    n = s_t.shape[1]
    tl = _tile(n, 256)
    fac = pl.BlockSpec((nk, tl), lambda i, h: (h, i))
    return pl.pallas_call(
        functools.partial(_peer_select_kernel, nk=nk),
        out_shape=[jax.ShapeDtypeStruct((heads * nk, n), F32), jax.ShapeDtypeStruct((heads * nk, n), F32),
                   jax.ShapeDtypeStruct((heads, 1, n), F32)],
        grid=(n // tl, heads),
        in_specs=[pl.BlockSpec((2 * nk, tl), lambda i, h: (h, i))],
        out_specs=[fac, fac, pl.BlockSpec((None, 1, tl), lambda i, h: (h, 0, i))],
        compiler_params=_cparams("parallel", "parallel"),
        name="peer_select",
    )(s_t)


def _peer_expert_kernel(xt_ref, s_ref, a1_ref, b2_ref, tau_ref, u_ref, v_ref, o_ref, *, heads, nk, te):
    e = pl.program_id(1)

    @pl.when(e == 0)
    def _():
        o_ref[...] = jnp.zeros(o_ref.shape, F32)

    h_t = jnp.dot(u_ref[...], xt_ref[...], preferred_element_type=F32)
    tm = h_t.shape[1]
    parts = []
    for a in range(te // nk):
        i1 = e * (te // nk) + a
        w = jnp.zeros((nk, tm), F32)
        for h in range(heads):
            s1 = s_ref[pl.ds(h * 2 * nk + i1, 1), :]
            a1 = a1_ref[pl.ds(h * nk + i1, 1), :]
            s2 = s_ref[h * 2 * nk + nk:(h + 1) * 2 * nk, :]
            b2 = b2_ref[h * nk:(h + 1) * nk, :]
            w = w + jnp.where((s2 + s1) >= tau_ref[h], b2, 0.0) * a1
        parts.append(w)
    gate = parts[0] if len(parts) == 1 else jnp.concatenate(parts, axis=0)
    act = 0.5 * h_t * (1.0 + lax.erf(h_t * (2.0 ** -0.5))) * gate
    o_ref[...] += lax.dot_general(act.astype(BF16), v_ref[...], TN, preferred_element_type=F32)


def peer_experts(x_t, s_t, a1, b2, tau, u_tab, v_tab, *, heads, nk):
    d, n = x_t.shape
    ne = u_tab.shape[0]
    tm = _tile(n, 512)
    te = 2 * nk
    tok = lambda rows: pl.BlockSpec((rows, tm), lambda i, e: (0, i))
    return pl.pallas_call(
        functools.partial(_peer_expert_kernel, heads=heads, nk=nk, te=te),
        out_shape=jax.ShapeDtypeStruct((n, d), F32),
        grid=(n // tm, ne // te),
        in_specs=[tok(d), tok(heads * 2 * nk), tok(heads * nk), tok(heads * nk),
                  pl.BlockSpec((heads, 1, tm), lambda i, e: (0, 0, i)),
                  pl.BlockSpec((te, d), lambda i, e: (e, 0)), pl.BlockSpec((te, d), lambda i, e: (e, 0))],
        out_specs=pl.BlockSpec((tm, d), lambda i, e: (i, 0)),
        compiler_params=_cparams("parallel", "arbitrary"),
        name="peer_experts",
    )(x_t, s_t, a1, b2, tau, u_tab, v_tab)


def peer_layer(h_parts, ln_g, wq_t, keys, u_tab, v_tab, *, heads, nk):
    x_t = rms_norm(h_parts, ln_g, transpose_out=True)
    s_t = matmul(wq_t, x_t, mode="keys", extra=(keys,), tm=2 * nk, tn=512, nk=nk)
    a1, b2, tau = peer_select(s_t, heads=heads, nk=nk)
    return peer_experts(x_t, s_t, a1, b2, tau, u_tab, v_tab, heads=heads, nk=nk)


def _paged_update(s, v, m_s, l_s, acc_s):
    m_prev = m_s[...]
    m_new = jnp.maximum(m_prev, jnp.max(s, axis=1, keepdims=True))
    alpha = jnp.exp(m_prev - m_new)
    p = jnp.exp(s - m_new)
    l_s[...] = alpha * l_s[...] + jnp.sum(p, axis=1, keepdims=True)
    acc_s[...] = alpha * acc_s[...] + jnp.dot(p.astype(BF16), v.astype(BF16), preferred_element_type=F32)
    m_s[...] = m_new


def _new_token_mask(rows, t_new):
    r = lax.broadcasted_iota(jnp.int32, (rows, PAGE), 0)
    u = lax.broadcasted_iota(jnp.int32, (rows, PAGE), 1)
    return (u <= (r % SUBLANES)) & (u < t_new)


def _paged_diff_kernel(pt_ref, lam_ref, q_ref, kc_ref, vc_ref, kn_ref, vn_ref, g_ref, o_ref,
                       m_s, l_s, acc_s, *, heads, t_new, post):
    p = pl.program_id(1)
    rows = q_ref.shape[0]
    scale = A_DH ** -0.5

    @pl.when(p == 0)
    def _():
        _flash_init(m_s, l_s, acc_s)
        s = lax.dot_general(q_ref[...], kn_ref[...].astype(BF16), NT, preferred_element_type=F32) * scale
        _paged_update(jnp.where(_new_token_mask(rows, t_new), s, -jnp.inf), vn_ref[...], m_s, l_s, acc_s)

    @pl.when(p > 0)
    def _():
        s = lax.dot_general(q_ref[...], kc_ref[...].astype(BF16), NT, preferred_element_type=F32) * scale
        _paged_update(s, vc_ref[...], m_s, l_s, acc_s)

    @pl.when(p == pl.num_programs(1) - 1)
    def _():
        dv = 2 * A_DH
        rb = SUBLANES
        for h in range(heads):
            r0 = 2 * h * rb
            c1 = acc_s[r0:r0 + rb, h * dv:(h + 1) * dv] / l_s[r0:r0 + rb, :]
            c2 = acc_s[r0 + rb:r0 + 2 * rb, h * dv:(h + 1) * dv] / l_s[r0 + rb:r0 + 2 * rb, :]
            o = c1 - lam_ref[0] * c2
            y = o * lax.rsqrt(jnp.mean(o * o, axis=-1, keepdims=True) + EPS) * g_ref[...]
            o_ref[:, h * dv:(h + 1) * dv] = (y * post).astype(o_ref.dtype)


def _block_diag_queries(q, groups, width, t_new):
    nseq = q.shape[0] // t_new
    q4 = q.reshape(nseq, t_new, groups, width)
    q4 = jnp.pad(q4, ((0, 0), (0, SUBLANES - t_new), (0, 0), (0, 0)))
    eye = jnp.eye(groups, dtype=q.dtype)
    out = jnp.einsum("stgw,gh->sgthw", q4, eye)
    return out.reshape(nseq, groups * SUBLANES, groups * width).astype(BF16)


def _pad_new_rows(x, nseq, t_new):
    x = x.reshape(nseq, t_new, x.shape[-1])
    return jnp.pad(x, ((0, 0), (0, PAGE - t_new), (0, 0)))


def diff_attention_paged(qa, ka, va, cache_k, cache_v, page_table, lam, subln_g, post, *, heads, t_new):
    nseq, n_pages = page_table.shape
    w = heads * 2 * A_DH
    rows = heads * 2 * SUBLANES
    qbd = _block_diag_queries(qa, heads * 2, A_DH, t_new)
    kn, vn = _pad_new_rows(ka, nseq, t_new), _pad_new_rows(va, nseq, t_new)
    page = lambda s, p, pt: (pt[s, jnp.maximum(p - 1, 0)], 0, 0)
    per_seq = lambda s, p, pt: (s, 0, 0)
    out = pl.pallas_call(
        functools.partial(_paged_diff_kernel, heads=heads, t_new=t_new, post=post),
        out_shape=jax.ShapeDtypeStruct((nseq, SUBLANES, w), BF16),
        grid_spec=pltpu.PrefetchScalarGridSpec(
            num_scalar_prefetch=1,
            grid=(nseq, n_pages + 1),
            in_specs=[pl.BlockSpec(memory_space=pltpu.SMEM),
                      pl.BlockSpec((None, rows, w), per_seq),
                      pl.BlockSpec((None, PAGE, w), page), pl.BlockSpec((None, PAGE, w), page),
                      pl.BlockSpec((None, PAGE, w), per_seq), pl.BlockSpec((None, PAGE, w), per_seq),
                      pl.BlockSpec((1, 2 * A_DH), lambda s, p, pt: (0, 0))],
            out_specs=pl.BlockSpec((None, SUBLANES, w), per_seq),
            scratch_shapes=[pltpu.VMEM((rows, 1), F32), pltpu.VMEM((rows, 1), F32), pltpu.VMEM((rows, w), F32)]),
        compiler_params=_cparams("parallel", "arbitrary"),
        name="diff_attention_paged",
    )(page_table, lam.reshape(1).astype(F32), qbd, cache_k, cache_v, kn, vn,
      subln_g.reshape(1, 2 * A_DH).astype(F32))
    return out[:, :t_new].reshape(nseq * t_new, w)


def _paged_fox_kernel(pt_ref, q_ref, kc_ref, vc_ref, lf_ref, kn_ref, vn_ref, npre_ref, nrow_ref, o_ref,
                      m_s, l_s, acc_s, suf_s, *, heads, dh, t_new):
    p = pl.program_id(1)
    rows = q_ref.shape[0]
    scale = dh ** -0.5

    @pl.when(p == 0)
    def _():
        _flash_init(m_s, l_s, acc_s)
        suf_s[...] = jnp.zeros(suf_s.shape, F32)
        s = lax.dot_general(q_ref[...], kn_ref[...].astype(BF16), NT, preferred_element_type=F32) * scale
        s = s + (npre_ref[...] - nrow_ref[...])
        _paged_update(jnp.where(_new_token_mask(rows, t_new), s, -jnp.inf), vn_ref[...], m_s, l_s, acc_s)

    @pl.when(p > 0)
    def _():
        r = lax.broadcasted_iota(jnp.int32, (rows, heads), 0)
        hh = lax.broadcasted_iota(jnp.int32, (rows, heads), 1)
        expand = ((r // SUBLANES) == hh).astype(F32)
        lf = lax.dot_general(expand, lf_ref[...], NT, precision=HI, preferred_element_type=F32)
        a = lax.broadcasted_iota(jnp.int32, (PAGE, PAGE), 0)
        b = lax.broadcasted_iota(jnp.int32, (PAGE, PAGE), 1)
        later = jnp.dot(lf, (a > b).astype(F32), precision=HI, preferred_element_type=F32)
        s = lax.dot_general(q_ref[...], kc_ref[...].astype(BF16), NT, preferred_element_type=F32) * scale
        s = s + (later + suf_s[...] + npre_ref[...])
        _paged_update(s, vc_ref[...], m_s, l_s, acc_s)
        suf_s[...] = suf_s[...] + jnp.sum(lf, axis=1, keepdims=True)

    @pl.when(p == pl.num_programs(1) - 1)
    def _():
        rb = SUBLANES
        for h in range(heads):
            o_ref[:, h * dh:(h + 1) * dh] = (acc_s[h * rb:(h + 1) * rb, h * dh:(h + 1) * dh]
                                             / l_s[h * rb:(h + 1) * rb, :]).astype(o_ref.dtype)


def fox_attention_paged(q, k, v, logf_new, cache_k, cache_v, cache_lf, page_table, *, heads, dh, t_new):
    nseq, n_pages = page_table.shape
    w = heads * dh
    rows = heads * SUBLANES
    qbd = _block_diag_queries(q, heads, dh, t_new)
    kn, vn = _pad_new_rows(k, nseq, t_new), _pad_new_rows(v, nseq, t_new)
    cs = jnp.cumsum(logf_new.reshape(nseq, t_new, heads).astype(F32), axis=1)
    cs8 = jnp.pad(cs, ((0, 0), (0, SUBLANES - t_new), (0, 0)))
    npre = jnp.transpose(cs8, (0, 2, 1)).reshape(nseq, rows, 1)
    csp = jnp.pad(cs, ((0, 0), (0, PAGE - t_new), (0, 0)))
    nrow = jnp.repeat(jnp.transpose(csp, (0, 2, 1)), SUBLANES, axis=1)
    page = lambda s, p, pt: (pt[s, n_pages - jnp.maximum(p, 1)], 0, 0)
    per_seq = lambda s, p, pt: (s, 0, 0)
    out = pl.pallas_call(
        functools.partial(_paged_fox_kernel, heads=heads, dh=dh, t_new=t_new),
        out_shape=jax.ShapeDtypeStruct((nseq, SUBLANES, w), BF16),
        grid_spec=pltpu.PrefetchScalarGridSpec(
            num_scalar_prefetch=1,
            grid=(nseq, n_pages + 1),
            in_specs=[pl.BlockSpec((None, rows, w), per_seq),
                      pl.BlockSpec((None, PAGE, w), page), pl.BlockSpec((None, PAGE, w), page),
                      pl.BlockSpec((None, PAGE, heads), page),
                      pl.BlockSpec((None, PAGE, w), per_seq), pl.BlockSpec((None, PAGE, w), per_seq),
                      pl.BlockSpec((None, rows, 1), per_seq), pl.BlockSpec((None, rows, PAGE), per_seq)],
            out_specs=pl.BlockSpec((None, SUBLANES, w), per_seq),
            scratch_shapes=[pltpu.VMEM((rows, 1), F32), pltpu.VMEM((rows, 1), F32), pltpu.VMEM((rows, w), F32),
                            pltpu.VMEM((rows, 1), F32)]),
        compiler_params=_cparams("parallel", "arbitrary"),
        name="fox_attention_paged",
    )(page_table, qbd, cache_k, cache_v, cache_lf, kn, vn, npre, nrow)
    return out[:, :t_new].reshape(nseq * t_new, w)


def _rope_tables(pos):
    rot = A_DH // 4
    half = rot // 2
    inv = jnp.power(ROPE_THETA, -jnp.arange(0, rot, 2, dtype=F32) / rot)
    ang = pos.astype(F32)[:, None] * inv[None, :]
    cos, sin = jnp.cos(ang), jnp.sin(ang)
    n = pos.shape[0]
    ones = jnp.ones((n, A_DH - rot), F32)
    zeros = jnp.zeros((n, A_DH - rot), F32)
    zh = jnp.zeros((n, half), F32)
    c = jnp.concatenate([cos, cos, ones], axis=1)
    sp = jnp.concatenate([-sin, zh, zeros], axis=1)
    sm = jnp.concatenate([zh, sin, zeros], axis=1)
    rep = LANES // A_DH
    return jnp.tile(c, (1, rep)), jnp.tile(sp, (1, rep)), jnp.tile(sm, (1, rep))


def _pad_rows(x, m):
    return x if x.shape[0] == m else jnp.pad(x, ((0, m - x.shape[0]), (0, 0)))


def _even_layer(h, pos, n_tok, attend, conv_buf, rec0, gdn_shape, w, lam, lam_init):
    heads_a, heads_b = w["heads_a"], w["heads_b"]
    xn = rms_norm([h], w["ln_mix"])
    tabs = tuple(_pad_rows(t, h.shape[0]) for t in _rope_tables(pos))
    qa = matmul(xn, w["w_qa"], mode="rope", extra=tabs)
    ka = matmul(xn, w["w_ka"], mode="rope", extra=tabs)
    va = matmul(xn, w["w_va"])
    qkv = matmul(xn, w["w_qkv"])
    z = matmul(xn, w["w_z"])
    ab = matmul(xn, w["w_ab"])
    att = attend(qa[:n_tok], ka[:n_tok], va[:n_tok], lam, w["subln_g"], 1.0 - lam_init)
    bsz, seq = gdn_shape
    seq_p = -(-seq // DN_CHUNK) * DN_CHUNK

    def seq_pad(x):
        x = x[:n_tok].reshape(bsz, seq, x.shape[-1])
        if seq_p != seq:
            x = jnp.pad(x, ((0, 0), (0, seq_p - seq), (0, 0)))
        return x

    gate = lambda x: jnp.transpose(seq_pad(x), (0, 2, 1))[..., None]
    buf8 = jnp.pad(conv_buf.astype(F32), ((0, 0), (SUBLANES - conv_buf.shape[1], 0), (0, 0)))
    dn, rec = gated_deltanet(
        seq_pad(qkv).reshape(bsz * seq_p, -1), seq_pad(z).reshape(bsz * seq_p, -1),
        gate(ab[:, :heads_b]), gate(ab[:, heads_b:2 * heads_b]), buf8, rec0.astype(F32),
        w["conv_w"], w["a_log"], w["dt_bias"], w["dn_norm_g"],
        bsz=bsz, seq=seq_p, heads=heads_b, valid=min(seq, DN_CHUNK))
    dn = dn.reshape(bsz, seq_p, -1)[:, :seq].reshape(n_tok, -1)
    xp = jnp.concatenate([conv_buf.astype(F32), qkv[:n_tok].reshape(bsz, seq, -1)], axis=1)
    conv_new = xp[:, seq:]
    m = h.shape[0]
    h = matmul(_pad_rows(att, m), w["w_out_a"], mode="resid", extra=(h,))
    h = matmul(_pad_rows(dn, m), w["w_out_b"], mode="resid", extra=(h,))
    return h, ka[:n_tok], va[:n_tok], conv_new, rec


def _odd_layer(h_parts, n_tok, attend, w):
    h, xn = rms_norm(h_parts, w["ln_mix"], emit_sum=True)
    q = matmul(xn, w["w_q"])
    k = matmul(xn, w["w_k"])
    v = matmul(xn, w["w_v"])
    logf = matmul(xn, w["w_f"], mode="logsig", extra=(w["b_f"],))[:, :w["heads_c"]]
    ctx = attend(q[:n_tok], k[:n_tok], v[:n_tok], logf[:n_tok])
    h = matmul(_pad_rows(ctx, h.shape[0]), w["w_out"], mode="resid", extra=(h,))
    return h, k[:n_tok], v[:n_tok], logf[:n_tok]


def kernel(x_prompt, x_sample, cache_a_k, cache_a_v, state_b_conv, state_b_rec, cache_c_k, cache_c_v, cache_c_logf, page_table, ln_mix, ln_ffn, ln_out, w_in_even, w_out_even, lam_q1, lam_k1, lam_q2, lam_k2, subln_g, conv_w, a_log, dt_bias, dn_norm_g, w_in_odd, b_forget, w_out_odd, peer_wq, peer_keys, peer_u, peer_v):
    bsz, seq, d = x_prompt.shape
    dbsz, dseq, _ = x_sample.shape
    depth = ln_mix.shape[0]
    heads_a, a_w = cache_a_k.shape[3], cache_a_k.shape[3] * cache_a_k.shape[4]
    heads_b, dk = state_b_rec.shape[2], state_b_rec.shape[3]
    heads_c, c_dh = cache_c_k.shape[3], cache_c_k.shape[4]
    c_w = heads_c * c_dh
    nk, dq = peer_keys.shape[2], peer_keys.shape[3]
    peer_heads = peer_wq.shape[-1] // (2 * dq)
    past_len = page_table.shape[1] * PAGE
    n_p, n_s = bsz * seq, dbsz * dseq
    m_s = -(-n_s // LANES) * LANES

    hp = x_prompt.reshape(n_p, d).astype(F32)
    hs = _pad_rows(x_sample.reshape(n_s, d).astype(F32), m_s)
    pos_p = jnp.tile(jnp.arange(seq), bsz)
    pos_s = jnp.tile(past_len + jnp.arange(dseq), dbsz)
    bf = lambda x: x.astype(BF16)
    lane_pad = lambda x: jnp.pad(x, ((0, 0), (0, LANES - x.shape[1])))

    outs_p = {k: [] for k in ("ak", "av", "bc", "br", "ck", "cv", "cf")}
    outs_s = {k: [] for k in ("ak", "av", "bc", "br", "ck", "cv", "cf")}
    peer_p = peer_s = None
    for li in range(depth):
        if li % 2 == 0:
            e = li // 2
            lam_init = 0.8 - 0.6 * math.exp(-0.3 * li)
            lam = (jnp.exp(jnp.sum(lam_q1[e] * lam_k1[e]).astype(F32))
                   - jnp.exp(jnp.sum(lam_q2[e] * lam_k2[e]).astype(F32)) + lam_init)
            wi, wo = w_in_even[e], w_out_even[e]
            o1, o2, o3 = a_w, 2 * a_w, 3 * a_w
            o4 = o3 + 3 * heads_b * dk
            o5 = o4 + heads_b * dk
            w = dict(heads_a=heads_a, heads_b=heads_b, ln_mix=ln_mix[li], subln_g=subln_g[e],
                     w_qa=bf(wi[:, :o1]), w_ka=bf(wi[:, o1:o2]), w_va=bf(wi[:, o2:o3]), w_qkv=bf(wi[:, o3:o4]),
                     w_z=bf(wi[:, o4:o5]), w_ab=bf(lane_pad(wi[:, o5:])),
                     w_out_a=bf(wo[:a_w]), w_out_b=bf(wo[a_w:]),
                     conv_w=conv_w[e].astype(F32), a_log=a_log[e], dt_bias=dt_bias[e], dn_norm_g=dn_norm_g[e])
            if peer_p is not None:
                hp, hs = hp + peer_p, hs + peer_s
            attend_p = functools.partial(diff_attention_prompt, bsz=bsz, seq=seq, heads=heads_a)
            hp, ka, va, cb, rs = _even_layer(
                hp, pos_p, n_p, attend_p, jnp.zeros((bsz, CONV_TAPS - 1, 3 * heads_b * dk), F32),
                jnp.zeros((bsz, heads_b, dk, dk), F32), (bsz, seq), w, lam, lam_init)
            for key, val in zip(("ak", "av", "bc", "br"), (ka, va, cb, rs)):
                outs_p[key].append(val)
            attend_s = functools.partial(
                diff_attention_paged, cache_k=cache_a_k[e].reshape(-1, PAGE, a_w),
                cache_v=cache_a_v[e].reshape(-1, PAGE, a_w), page_table=page_table, heads=heads_a, t_new=dseq)
            attend_s2 = lambda q, k, v, lam_, g_, post_: attend_s(q, k, v, lam=lam_, subln_g=g_, post=post_)
            hs, ka, va, cb, rs = _even_layer(
                hs, pos_s, n_s, attend_s2, state_b_conv[e], state_b_rec[e], (dbsz, dseq), w, lam, lam_init)
            for key, val in zip(("ak", "av", "bc", "br"), (ka, va, cb, rs)):
                outs_s[key].append(val)
            parts_p, parts_s = [hp], [hs]
        else:
            o = li // 2
            wi = w_in_odd[o]
            w = dict(heads_c=heads_c, ln_mix=ln_mix[li], w_q=bf(wi[:, :c_w]), w_k=bf(wi[:, c_w:2 * c_w]),
                     w_v=bf(wi[:, 2 * c_w:3 * c_w]), w_f=bf(lane_pad(wi[:, 3 * c_w:])),
                     b_f=lane_pad(b_forget[o].reshape(1, heads_c).astype(F32)), w_out=bf(w_out_odd[o]))

            def attend_p(q, k, v, logf):
                f = cumsum_time(lane_pad(logf).reshape(bsz, seq, LANES))[:, :, :heads_c]
                f_t = jnp.transpose(f, (0, 2, 1))
                return fox_attention_prompt(q, k, v, f_t[..., None], f_t[:, :, None, :],
                                            bsz=bsz, seq=seq, heads=heads_c, dh=c_dh)

            attend_s = functools.partial(
                fox_attention_paged, cache_k=cache_c_k[o].reshape(-1, PAGE, c_w),
                cache_v=cache_c_v[o].reshape(-1, PAGE, c_w), cache_lf=cache_c_logf[o].astype(F32),
                page_table=page_table, heads=heads_c, dh=c_dh, t_new=dseq)
            parts_p = [hp] if peer_p is None else [hp, peer_p]
            parts_s = [hs] if peer_s is None else [hs, peer_s]
            hp, kc, vc, lf = _odd_layer(parts_p, n_p, attend_p, w)
            for key, val in zip(("ck", "cv", "cf"), (kc, vc, lf)):
                outs_p[key].append(val)
            hs, kc, vc, lf = _odd_layer(
                parts_s, n_s, lambda q, k, v, logf: attend_s(q, k, v, logf), w)
            for key, val in zip(("ck", "cv", "cf"), (kc, vc, lf)):
                outs_s[key].append(val)
            parts_p, parts_s = [hp], [hs]
        wq_t = bf(peer_wq[li].T)
        keys = bf(peer_keys[li])
        u_tab, v_tab = bf(peer_u[li]), bf(peer_v[li])
        peer_p = peer_layer(parts_p, ln_ffn[li], wq_t, keys, u_tab, v_tab, heads=peer_heads, nk=nk)
        peer_s = peer_layer(parts_s, ln_ffn[li], wq_t, keys, u_tab, v_tab, heads=peer_heads, nk=nk)

    y_p = rms_norm([hp, peer_p], ln_out, out_dtype=F32).reshape(bsz, seq, d)
    y_s = rms_norm([hs, peer_s], ln_out, out_dtype=F32)[:n_s].reshape(dbsz, dseq, d)

    def stack(vals, *shape):
        return jnp.stack([v.reshape(*shape) for v in vals])

    def group(outs, b, t):
        return (stack(outs["ak"], b, t, heads_a, a_w // heads_a), stack(outs["av"], b, t, heads_a, a_w // heads_a),
                stack(outs["bc"], b, CONV_TAPS - 1, 3 * heads_b * dk), stack(outs["br"], b, heads_b, dk, dk),
                stack(outs["ck"], b, t, heads_c, c_dh), stack(outs["cv"], b, t, heads_c, c_dh),
                stack(outs["cf"], b, t, heads_c))

    return (y_p, y_s) + group(outs_p, bsz, seq) + group(outs_s, dbsz, dseq)
```

```python
import functools
import math

import jax
import jax.numpy as jnp
from jax import lax
from jax.experimental import pallas as pl
from jax.experimental.pallas import tpu as pltpu

F32 = jnp.float32
BF16 = jnp.bfloat16
HI = lax.Precision.HIGHEST

EPS = 1e-6
A_DH = 64
ROPE_THETA = 500000.0
CONV_TAPS = 4
DN_CHUNK = 64
GDN_HEADS_PER_STEP = 8
FOX_HEADS_PER_STEP = 2
PEER_TOPK = 16
PEER_TOKEN_SUB = 256
PAGE = 128
LANES = 128
SUBLANES = 8
VMEM_LIMIT = 56 * 1024 * 1024

NT = (((1,), (1,)), ((), ()))
TN = (((0,), (0,)), ((), ()))


def _cparams(*sem):
    return pltpu.CompilerParams(dimension_semantics=sem, vmem_limit_bytes=VMEM_LIMIT)


def _tile(n, pref):
    if n <= pref:
        return n
    t = pref
    while n % t:
        t //= 2
    return t


def _dot1(a, b, dims=None):
    a, b = a.astype(BF16), b.astype(BF16)
    if dims is None:
        return jnp.dot(a, b, preferred_element_type=F32)
    return lax.dot_general(a, b, dims, preferred_element_type=F32)


def _dot3(a, b, dims=None):
    ah, bh = a.astype(BF16), b.astype(BF16)
    al, bl = (a - ah.astype(F32)).astype(BF16), (b - bh.astype(F32)).astype(BF16)
    return _dot1(ah, bh, dims) + _dot1(ah, bl, dims) + _dot1(al, bh, dims)


def _rms_kernel(*refs, n_in, emit_sum, transpose_out):
    x = refs[0][...]
    for r in refs[1:n_in]:
        x = x + r[...]
    g = refs[n_in][...]
    outs = refs[n_in + 1:]
    y = x * lax.rsqrt(jnp.mean(x * x, axis=-1, keepdims=True) + EPS) * g
    k = 0
    if emit_sum:
        outs[0][...] = x
        k = 1
    if transpose_out:
        outs[k][...] = y.T.astype(outs[k].dtype)
    else:
        outs[k][...] = y.astype(outs[k].dtype)


def rms_norm(xs, g, *, out_dtype=BF16, emit_sum=False, transpose_out=False):
    m, d = xs[0].shape
    tm = _tile(m, 256)
    row = pl.BlockSpec((tm, d), lambda i: (i, 0))
    out_shape, out_specs = [], []
    if emit_sum:
        out_shape.append(jax.ShapeDtypeStruct((m, d), F32))
        out_specs.append(row)
    if transpose_out:
        out_shape.append(jax.ShapeDtypeStruct((d, m), out_dtype))
        out_specs.append(pl.BlockSpec((d, tm), lambda i: (0, i)))
    else:
        out_shape.append(jax.ShapeDtypeStruct((m, d), out_dtype))
        out_specs.append(row)
    res = pl.pallas_call(
        functools.partial(_rms_kernel, n_in=len(xs), emit_sum=emit_sum, transpose_out=transpose_out),
        out_shape=out_shape,
        grid=(m // tm,),
        in_specs=[row] * len(xs) + [pl.BlockSpec((1, d), lambda i: (0, 0))],
        out_specs=out_specs,
        compiler_params=_cparams("parallel"),
        name="rms_norm",
    )(*xs, g.reshape(1, d).astype(F32))
    return res if emit_sum else res[0]


def _mm_kernel(*refs, mode, nk):
    a_ref, b_ref = refs[0], refs[1]
    o_ref = refs[-1]
    acc = jnp.dot(a_ref[...], b_ref[...], preferred_element_type=F32)
    if mode == "plain":
        o_ref[...] = acc
    elif mode == "resid":
        o_ref[...] = acc + refs[2][...]
    elif mode == "rope":
        c, sp, sm = refs[2][...], refs[3][...], refs[4][...]
        for g in range(acc.shape[1] // LANES):
            x = acc[:, g * LANES:(g + 1) * LANES]
            o_ref[:, g * LANES:(g + 1) * LANES] = (
                x * c + pltpu.roll(x, LANES - 8, axis=1) * sp + pltpu.roll(x, 8, axis=1) * sm)
    elif mode == "logsig":
        x = acc + refs[2][...]
        o_ref[...] = jnp.minimum(x, 0.0) - jnp.log1p(jnp.exp(-jnp.abs(x)))
    elif mode == "keys":
        keys_ref = refs[2]
        for c in range(2):
            q = acc[c * nk:(c + 1) * nk, :].astype(BF16)
            o_ref[c * nk:(c + 1) * nk, :] = jnp.dot(keys_ref[c], q, preferred_element_type=F32)
    else:
        raise ValueError(mode)


def matmul(a, b, *, mode="plain", extra=(), tm=512, tn=512, nk=0):
    m, k = a.shape
    _, n = b.shape
    tm = _tile(m, tm)
    tn = _tile(n, tn)
    in_specs = [pl.BlockSpec((tm, k), lambda i, j: (i, 0)),
                pl.BlockSpec((k, tn), lambda i, j: (0, j))]
    if mode == "resid":
        in_specs.append(pl.BlockSpec((tm, tn), lambda i, j: (i, j)))
    elif mode == "rope":
        in_specs += [pl.BlockSpec((tm, LANES), lambda i, j: (i, 0))] * 3
    elif mode == "logsig":
        in_specs.append(pl.BlockSpec((1, tn), lambda i, j: (0, j)))
    elif mode == "keys":
        in_specs.append(pl.BlockSpec(extra[0].shape, lambda i, j: (0, 0, 0)))
    return pl.pallas_call(
        functools.partial(_mm_kernel, mode=mode, nk=nk),
        out_shape=jax.ShapeDtypeStruct((m, n), F32),
        grid=(m // tm, n // tn),
        in_specs=in_specs,
        out_specs=pl.BlockSpec((tm, tn), lambda i, j: (i, j)),
        compiler_params=_cparams("parallel", "parallel"),
        name="matmul_" + mode,
    )(a, b, *extra)


def _flash_init(m_s, l_s, acc_s):
    m_s[...] = jnp.full(m_s.shape, -jnp.inf, F32)
    l_s[...] = jnp.zeros(l_s.shape, F32)
    acc_s[...] = jnp.zeros(acc_s.shape, F32)


def _lane_tile(x, width):
    return x if width == LANES else jnp.tile(x, (1, width // LANES))


def _flash_update(ss, vs, m_s, l_s, acc_s):
    n = range(len(ss))
    width = ss[0].shape[1]
    m_prev = [m_s[c] for c in n]
    m_new = [jnp.maximum(m_prev[c], jnp.max(ss[c], axis=1, keepdims=True)) for c in n]
    p = [jnp.exp(ss[c] - _lane_tile(m_new[c], width)) for c in n]
    pv = [jnp.dot(p[c].astype(BF16), vs[c], preferred_element_type=F32) for c in n]
    for c in n:
        alpha = jnp.exp(m_prev[c] - m_new[c])
        l_s[c] = alpha * l_s[c] + jnp.sum(p[c], axis=1, keepdims=True)
        acc_s[c] = alpha * acc_s[c] + pv[c]
        m_s[c] = m_new[c]


def _lower_triangle(tq, tk):
    return lax.broadcasted_iota(jnp.int32, (tq, tk), 1) <= lax.broadcasted_iota(jnp.int32, (tq, tk), 0)


def _diff_flash_kernel(lam_ref, q_ref, k_ref, v_ref, g_ref, o_ref, m_s, l_s, acc_s, *, tq, tk, post):
    i, j = pl.program_id(2), pl.program_id(3)

    @pl.when(j == 0)
    def _():
        _flash_init(m_s, l_s, acc_s)

    def step(masked):
        q = (q_ref[...] * (A_DH ** -0.5)).astype(BF16)
        k = k_ref[...].astype(BF16)
        v = v_ref[...].astype(BF16)
        ss = [lax.dot_general(q[:, c * A_DH:(c + 1) * A_DH], k[:, c * A_DH:(c + 1) * A_DH], NT,
                              preferred_element_type=F32) for c in range(2)]
        if masked:
            ss = [jnp.where(_lower_triangle(tq, tk), s, -jnp.inf) for s in ss]
        _flash_update(ss, [v, v], m_s, l_s, acc_s)

    pl.when(j < i)(functools.partial(step, False))
    pl.when(j == i)(functools.partial(step, True))

    @pl.when(j == pl.num_programs(3) - 1)
    def _():
        o = acc_s[0] / l_s[0] - lam_ref[0] * (acc_s[1] / l_s[1])
        y = o * lax.rsqrt(jnp.mean(o * o, axis=-1, keepdims=True) + EPS) * g_ref[...]
        o_ref[...] = (y * post).astype(o_ref.dtype)


def diff_attention_prompt(qa, ka, va, lam, subln_g, post, *, bsz, seq, heads):
    dv = 2 * A_DH
    tq = tk = _tile(seq, 512)
    nq = nk = seq // tq
    kv_map = lambda b, h, i, j: (b * nk + jnp.minimum(j, i), h)
    return pl.pallas_call(
        functools.partial(_diff_flash_kernel, tq=tq, tk=tk, post=post),
        out_shape=jax.ShapeDtypeStruct((bsz * seq, heads * dv), BF16),
        grid=(bsz, heads, nq, nk),
        in_specs=[pl.BlockSpec(memory_space=pltpu.SMEM),
                  pl.BlockSpec((tq, dv), lambda b, h, i, j: (b * nq + i, h)),
                  pl.BlockSpec((tk, dv), kv_map),
                  pl.BlockSpec((tk, dv), kv_map),
                  pl.BlockSpec((1, dv), lambda b, h, i, j: (0, 0))],
        out_specs=pl.BlockSpec((tq, dv), lambda b, h, i, j: (b * nq + i, h)),
        scratch_shapes=[pltpu.VMEM((2, tq, LANES), F32), pltpu.VMEM((2, tq, LANES), F32),
                        pltpu.VMEM((2, tq, dv), F32)],
        compiler_params=_cparams("parallel", "parallel", "parallel", "arbitrary"),
        name="diff_attention_prompt",
    )(lam.reshape(1).astype(F32), qa, ka, va, subln_g.reshape(1, dv).astype(F32))


def _fox_flash_kernel(q_ref, k_ref, v_ref, fk_ref, o_ref, m_s, l_s, acc_s, *, tq, tk, dh, hb, scale):
    i, j = pl.program_id(2), pl.program_id(3)

    @pl.when(j == 0)
    def _():
        _flash_init(m_s, l_s, acc_s)

    def step(masked):
        q = (q_ref[...] * scale).astype(BF16)
        k = k_ref[...].astype(BF16)
        v = v_ref[...].astype(BF16)
        cols = [slice(h * dh, (h + 1) * dh) for h in range(hb)]
        ss = [lax.dot_general(q[:, c], k[:, c], NT, preferred_element_type=F32) - fk_ref[h]
              for h, c in enumerate(cols)]
        if masked:
            ss = [jnp.where(_lower_triangle(tq, tk), s, -jnp.inf) for s in ss]
        _flash_update(ss, [v[:, c] for c in cols], m_s, l_s, acc_s)

    pl.when(j < i)(functools.partial(step, False))
    pl.when(j == i)(functools.partial(step, True))

    @pl.when(j == pl.num_programs(3) - 1)
    def _():
        for h in range(hb):
            o_ref[:, h * dh:(h + 1) * dh] = (acc_s[h] / l_s[h]).astype(o_ref.dtype)


def fox_attention_prompt(q, k, v, f_row, *, bsz, seq, heads, dh):
    tq = tk = _tile(seq, 512)
    nq = nk = seq // tq
    hb = math.gcd(heads, FOX_HEADS_PER_STEP)
    w = hb * dh
    kv_map = lambda b, g, i, j: (b * nk + jnp.minimum(j, i), g)
    return pl.pallas_call(
        functools.partial(_fox_flash_kernel, tq=tq, tk=tk, dh=dh, hb=hb, scale=dh ** -0.5),
        out_shape=jax.ShapeDtypeStruct((bsz * seq, heads * dh), BF16),
        grid=(bsz, heads // hb, nq, nk),
        in_specs=[pl.BlockSpec((tq, w), lambda b, g, i, j: (b * nq + i, g)),
                  pl.BlockSpec((tk, w), kv_map),
                  pl.BlockSpec((tk, w), kv_map),
                  pl.BlockSpec((None, hb, 1, tk), lambda b, g, i, j: (b, g, 0, jnp.minimum(j, i)))],
        out_specs=pl.BlockSpec((tq, w), lambda b, g, i, j: (b * nq + i, g)),
        scratch_shapes=[pltpu.VMEM((hb, tq, LANES), F32), pltpu.VMEM((hb, tq, LANES), F32),
                        pltpu.VMEM((hb, tq, dh), F32)],
        compiler_params=_cparams("parallel", "parallel", "parallel", "arbitrary"),
        name="fox_attention_prompt",
    )(q, k, v, f_row)


def _cumsum_kernel(x_ref, o_ref, carry_s, *, tc):
    @pl.when(pl.program_id(1) == 0)
    def _():
        carry_s[...] = jnp.zeros(carry_s.shape, F32)

    r = lax.broadcasted_iota(jnp.int32, (tc, tc), 0)
    c = lax.broadcasted_iota(jnp.int32, (tc, tc), 1)
    tri = (c <= r).astype(F32)
    y = jnp.dot(tri, x_ref[...], precision=HI, preferred_element_type=F32) + carry_s[...]
    o_ref[...] = y
    carry_s[...] = y[tc - 1:tc, :]


def cumsum_time(x):
    bsz, seq, w = x.shape
    tc = _tile(seq, 256)
    return pl.pallas_call(
        functools.partial(_cumsum_kernel, tc=tc),
        out_shape=jax.ShapeDtypeStruct(x.shape, F32),
        grid=(bsz, seq // tc),
        in_specs=[pl.BlockSpec((None, tc, w), lambda b, i: (b, i, 0))],
        out_specs=pl.BlockSpec((None, tc, w), lambda b, i: (b, i, 0)),
        scratch_shapes=[pltpu.VMEM((1, w), F32)],
        compiler_params=_cparams("parallel", "arbitrary"),
        name="cumsum_time",
    )(x)


def _each(f, *lists):
    return [f(*args) for args in zip(*lists)]


def _unit_lower_inverse(ns, size):
    ri = lax.broadcasted_iota(jnp.int32, (size, size), 0)
    ci = lax.broadcasted_iota(jnp.int32, (size, size), 1)
    eye = (ri == ci).astype(F32)
    bs = min(size, 16)
    nd = [jnp.where((ri // bs) == (ci // bs), n, 0.0) for n in ns]
    x, pw, p = [eye - m for m in nd], _each(_dot3, nd, nd), 2
    while p < bs:
        x = _each(lambda a, b: _dot3(a, eye + b), x, pw)
        if 2 * p < bs:
            pw = _each(_dot3, pw, pw)
        p *= 2
    if size == bs:
        return x
    nb = size // bs
    blk = _each(lambda a, n, m: _dot3(a, n - m), x, ns, nd)
    y = [eye - m for m in blk]
    if nb > 2:
        pw, p = _each(_dot3, blk, blk), 2
        while p < nb:
            y = _each(lambda a, b: _dot3(a, eye + b), y, pw)
            if 2 * p < nb:
                pw = _each(_dot3, pw, pw)
            p *= 2
    return _each(_dot3, y, x)


def _gdn_kernel(nea_ref, dtb_ref, xq_ref, xk_ref, xv_ref, z_ref, a_ref, b_ref,
                cq_ref, ck_ref, cv_ref, wq_ref, wk_ref, wv_ref, s0_ref, ng_ref,
                o_ref, sfin_ref, prev_s, state_s, *, chunk, valid, dk, hb):
    head0, c = pl.program_id(1) * hb, pl.program_id(2)
    halo = SUBLANES

    @pl.when(c == 0)
    def _():
        prev_s[0] = cq_ref[...]
        prev_s[1] = ck_ref[...]
        prev_s[2] = cv_ref[...]
        state_s[...] = s0_ref[...]

    def conv_act(idx, x_ref, w_ref):
        x = x_ref[...]
        w = w_ref[...]
        xx = jnp.concatenate([prev_s[idx], x], axis=0)
        y = x * w[CONV_TAPS - 1:CONV_TAPS, :]
        for s in range(1, CONV_TAPS):
            y = y + pltpu.roll(xx, s, axis=0)[halo:halo + chunk] * w[CONV_TAPS - 1 - s:CONV_TAPS - s, :]
        prev_s[idx] = x[chunk - halo:chunk]
        return y / (1.0 + jnp.exp(-y))

    aq_all, ak_all, v_all = conv_act(0, xq_ref, wq_ref), conv_act(1, xk_ref, wk_ref), conv_act(2, xv_ref, wv_ref)
    zz = z_ref[...]
    gate_out = zz / (1.0 + jnp.exp(-zz))
    ri = lax.broadcasted_iota(jnp.int32, (chunk, chunk), 0)
    ci = lax.broadcasted_iota(jnp.int32, (chunk, chunk), 1)
    incl = ci <= ri
    row_b = lax.broadcasted_iota(jnp.int32, (chunk, dk), 0)

    def prepare(i):
        sl = slice(i * dk, (i + 1) * dk)
        aq, ak, v = aq_all[:, sl], ak_all[:, sl], v_all[:, sl]
        q = aq * lax.rsqrt(jnp.sum(aq * aq, axis=-1, keepdims=True) + EPS) * (dk ** -0.5)
        k = ak * lax.rsqrt(jnp.sum(ak * ak, axis=-1, keepdims=True) + EPS)
        x = a_ref[i] + dtb_ref[head0 + i]
        g = nea_ref[head0 + i] * (jnp.maximum(x, 0.0) + jnp.log1p(jnp.exp(-jnp.abs(x))))
        beta = 1.0 / (1.0 + jnp.exp(-b_ref[i]))
        if valid < chunk:
            live = lax.broadcasted_iota(jnp.int32, (chunk, 1), 0) < valid
            g = jnp.where(live, g, 0.0)
            beta = jnp.where(live, beta, 0.0)
            k = jnp.where(live, k, 0.0)
        gc_b = jnp.broadcast_to(g, (chunk, dk))
        s = 1
        while s < chunk:
            gc_b = gc_b + jnp.where(row_b >= s, pltpu.roll(gc_b, s, axis=0), 0.0)
            s *= 2
        gc_row = jnp.concatenate([gc_b] * (LANES // chunk), axis=0).T[:chunk, :chunk]
        decay = jnp.exp(jnp.where(incl, gc_b[:, :chunk] - gc_row, -jnp.inf))
        return q, k, v * beta, k * beta, gc_b, decay

    heads = range(hb)
    q, k, vb, kb, gc_b, decay = zip(*[prepare(i) for i in heads])
    lower = _each(lambda a, b, d: jnp.where(ci < ri, _dot3(a, b, NT) * d, 0.0), kb, k, decay)
    intra = _each(lambda a, b, d: jnp.where(incl, _dot1(a, b, NT) * d, 0.0), q, k, decay)
    inv = _unit_lower_inverse(lower, chunk)
    egc = [jnp.exp(x) for x in gc_b]
    uw = _each(lambda t, a, b, e: _dot3(t, jnp.concatenate([a, b * e], axis=1)), inv, vb, kb, egc)
    state = [state_s[i] for i in heads]
    new_v = _each(lambda x, s: x[:, :dk] - _dot3(x[:, dk:], s), uw, state)
    o = _each(lambda a, e, s, t, nv: _dot1(a * e, s) + _dot1(t, nv), q, egc, state, intra, new_v)
    g_last = [x[chunk - 1:chunk, :] for x in gc_b]
    state = _each(lambda s, gl, a, gc, nv: s * jnp.exp(gl) + _dot3(a * jnp.exp(gl - gc), nv, TN),
                  state, g_last, k, gc_b, new_v)
    for i in heads:
        sl = slice(i * dk, (i + 1) * dk)
        state_s[i] = state[i]
        y = o[i] * lax.rsqrt(jnp.mean(o[i] * o[i], axis=-1, keepdims=True) + EPS) * ng_ref[...]
        o_ref[:, sl] = (y * gate_out[:, sl]).astype(o_ref.dtype)

    @pl.when(c == pl.num_programs(2) - 1)
    def _():
        sfin_ref[...] = state_s[...]


def gated_deltanet(qkv, z, a, b, conv_buf8, s0, conv_w, a_log, dt_bias, norm_g, *, bsz, seq, heads, valid):
    dk = LANES
    chunk = DN_CHUNK
    assert LANES % chunk == 0
    nch = seq // chunk
    assert nch == 1 or valid == chunk
    hb = math.gcd(heads, GDN_HEADS_PER_STEP)
    ng = heads // hb
    w = hb * dk
    col = lambda off: (lambda bb, g, c: (bb * nch + c, off + g))
    buf = lambda off: (lambda bb, g, c: (bb, 0, off + g))
    wsp = lambda off: (lambda bb, g, c: (0, off + g))
    gate = pl.BlockSpec((None, hb, chunk, 1), lambda bb, g, c: (bb, g, c, 0))
    st = pl.BlockSpec((None, hb, dk, dk), lambda bb, g, c: (bb, g, 0, 0))
    smem = pl.BlockSpec(memory_space=pltpu.SMEM)
    return pl.pallas_call(
        functools.partial(_gdn_kernel, chunk=chunk, valid=valid, dk=dk, hb=hb),
        out_shape=[jax.ShapeDtypeStruct((bsz * seq, heads * dk), BF16),
                   jax.ShapeDtypeStruct((bsz, heads, dk, dk), F32)],
        grid=(bsz, ng, nch),
        in_specs=[smem, smem,
                  pl.BlockSpec((chunk, w), col(0)), pl.BlockSpec((chunk, w), col(ng)),
                  pl.BlockSpec((chunk, w), col(2 * ng)), pl.BlockSpec((chunk, w), col(0)),
                  gate, gate,
                  pl.BlockSpec((None, SUBLANES, w), buf(0)), pl.BlockSpec((None, SUBLANES, w), buf(ng)),
                  pl.BlockSpec((None, SUBLANES, w), buf(2 * ng)),
                  pl.BlockSpec((CONV_TAPS, w), wsp(0)), pl.BlockSpec((CONV_TAPS, w), wsp(ng)),
                  pl.BlockSpec((CONV_TAPS, w), wsp(2 * ng)),
                  st, pl.BlockSpec((1, dk), lambda bb, g, c: (0, 0))],
        out_specs=[pl.BlockSpec((chunk, w), col(0)), st],
        scratch_shapes=[pltpu.VMEM((3, SUBLANES, w), F32), pltpu.VMEM((hb, dk, dk), F32)],
        compiler_params=_cparams("parallel", "parallel", "arbitrary"),
        name="gated_deltanet",
    )(-jnp.exp(a_log.astype(F32)), dt_bias.astype(F32), qkv, qkv, qkv, z, a, b,
      conv_buf8, conv_buf8, conv_buf8, conv_w, conv_w, conv_w, s0, norm_g.reshape(1, dk).astype(F32))


def _extract_top(s, count):
    rows = lax.broadcasted_iota(jnp.int32, s.shape, 0)
    work = s
    member = jnp.zeros(s.shape, F32)
    vals = []
    for _ in range(count):
        m = jnp.max(work, axis=0, keepdims=True)
        idx = jnp.min(jnp.where(work == m, rows, s.shape[0]), axis=0, keepdims=True)
        hit = rows == idx
        member = jnp.where(hit, 1.0, member)
        work = jnp.where(hit, -jnp.inf, work)
        vals.append(m)
    return vals, member > 0.5


def _peer_select_kernel(s_ref, a1_ref, b2_ref, tau_ref, *, nk):
    s1 = s_ref[0:nk, :]
    s2 = s_ref[nk:2 * nk, :]
    tl = s1.shape[1]
    v1, mem1 = _extract_top(s1, PEER_TOPK)
    v2, mem2 = _extract_top(s2, PEER_TOPK)
    sub = lax.broadcasted_iota(jnp.int32, (PEER_TOPK, tl), 0)
    v2_all = jnp.zeros((PEER_TOPK, tl), F32)
    for t in range(PEER_TOPK):
        v2_all = jnp.where(sub == t, v2[t], v2_all)
    cand = jnp.concatenate([v1[t] + v2_all for t in range(PEER_TOPK)], axis=0)
    top, _ = _extract_top(cand, PEER_TOPK)
    z = jnp.ones((1, tl), F32)
    for t in range(1, PEER_TOPK):
        z = z + jnp.exp(top[t] - top[0])
    a1_ref[...] = jnp.where(mem1, jnp.exp(s1 - v1[0]), 0.0) / z
    b2_ref[...] = jnp.where(mem2, jnp.exp(s2 - v2[0]), 0.0)
    tau_ref[...] = top[PEER_TOPK - 1]


def peer_select(s_t, *, heads, nk):
    n = s_t.shape[1]
    tl = _tile(n, 256)
    fac = pl.BlockSpec((nk, tl), lambda i, h: (h, i))
    return pl.pallas_call(
        functools.partial(_peer_select_kernel, nk=nk),
        out_shape=[jax.ShapeDtypeStruct((heads * nk, n), F32), jax.ShapeDtypeStruct((heads * nk, n), F32),
                   jax.ShapeDtypeStruct((heads, 1, n), F32)],
        grid=(n // tl, heads),
        in_specs=[pl.BlockSpec((2 * nk, tl), lambda i, h: (h, i))],
        out_specs=[fac, fac, pl.BlockSpec((None, 1, tl), lambda i, h: (h, 0, i))],
        compiler_params=_cparams("parallel", "parallel"),
        name="peer_select",
    )(s_t)


def _peer_expert_kernel(xt_ref, s_ref, a1_ref, b2_ref, tau_ref, u_ref, v_ref, o_ref, *, heads, nk, te):
    e = pl.program_id(1)

    @pl.when(e == 0)
    def _():
        o_ref[...] = jnp.zeros(o_ref.shape, F32)

    tm = xt_ref.shape[1]
    tsub = PEER_TOKEN_SUB if tm % PEER_TOKEN_SUB == 0 else tm
    subs = [slice(t * tsub, (t + 1) * tsub) for t in range(tm // tsub)]
    h_t = [jnp.dot(u_ref[...], xt_ref[:, c], preferred_element_type=F32) for c in subs]
    acts = []
    for c, hh in zip(subs, h_t):
        parts = []
        for a in range(te // nk):
            i1 = e * (te // nk) + a
            w = jnp.zeros((nk, tsub), F32)
            for h in range(heads):
                s1 = s_ref[pl.ds(h * 2 * nk + i1, 1), c]
                a1 = a1_ref[pl.ds(h * nk + i1, 1), c]
                s2 = s_ref[h * 2 * nk + nk:(h + 1) * 2 * nk, c]
                b2 = b2_ref[h * nk:(h + 1) * nk, c]
                w = w + jnp.where((s2 + s1) >= tau_ref[h][:, c], b2, 0.0) * a1
            parts.append(w)
        gate = parts[0] if len(parts) == 1 else jnp.concatenate(parts, axis=0)
        acts.append((0.5 * hh * (1.0 + lax.erf(hh * (2.0 ** -0.5))) * gate).astype(BF16))
    upd = [lax.dot_general(act, v_ref[...], TN, preferred_element_type=F32) for act in acts]
    for c, x in zip(subs, upd):
        o_ref[c, :] += x


def peer_experts(x_t, s_t, a1, b2, tau, u_tab, v_tab, *, heads, nk):
    d, n = x_t.shape
    ne = u_tab.shape[0]
    tm = _tile(n, 512)
    te = 4 * nk
    tok = lambda rows: pl.BlockSpec((rows, tm), lambda i, e: (0, i), pipeline_mode=pl.Buffered(1))
    return pl.pallas_call(
        functools.partial(_peer_expert_kernel, heads=heads, nk=nk, te=te),
        out_shape=jax.ShapeDtypeStruct((n, d), F32),
        grid=(n // tm, ne // te),
        in_specs=[tok(d), tok(heads * 2 * nk), tok(heads * nk), tok(heads * nk),
                  pl.BlockSpec((heads, 1, tm), lambda i, e: (0, 0, i), pipeline_mode=pl.Buffered(1)),
                  pl.BlockSpec((te, d), lambda i, e: (e, 0)), pl.BlockSpec((te, d), lambda i, e: (e, 0))],
        out_specs=pl.BlockSpec((tm, d), lambda i, e: (i, 0)),
        compiler_params=_cparams("parallel", "arbitrary"),
        name="peer_experts",
    )(x_t, s_t, a1, b2, tau, u_tab, v_tab)


def peer_layer(h_parts, ln_g, wq_t, keys, u_tab, v_tab, *, heads, nk):
    x_t = rms_norm(h_parts, ln_g, transpose_out=True)
    s_t = matmul(wq_t, x_t, mode="keys", extra=(keys,), tm=2 * nk, tn=512, nk=nk)
    a1, b2, tau = peer_select(s_t, heads=heads, nk=nk)
    return peer_experts(x_t, s_t, a1, b2, tau, u_tab, v_tab, heads=heads, nk=nk)


def _paged_update(s, v, m_s, l_s, acc_s):
    m_prev = m_s[...]
    m_new = jnp.maximum(m_prev, jnp.max(s, axis=1, keepdims=True))
    alpha = jnp.exp(m_prev - m_new)
    p = jnp.exp(s - _lane_tile(m_new, s.shape[1]))
    l_s[...] = alpha * l_s[...] + jnp.sum(p, axis=1, keepdims=True)
    acc_s[...] = alpha * acc_s[...] + jnp.dot(p.astype(BF16), v.astype(BF16), preferred_element_type=F32)
    m_s[...] = m_new


def _head_mask(rows, keys, heads, t_new, causal_new):
    r = lax.broadcasted_iota(jnp.int32, (rows, keys), 0)
    c = lax.broadcasted_iota(jnp.int32, (rows, keys), 1)
    ok = (c % heads) == ((r // t_new) % heads)
    if causal_new:
        ok = ok & ((c // heads) <= (r % t_new))
    return jnp.where(ok, 0.0, -jnp.inf)


def _paged_diff_kernel(pt_ref, lam_ref, q_ref, kc_ref, vc_ref, kn_ref, vn_ref, g_ref, o_ref,
                       m_s, l_s, acc_s, mask_s, *, heads, t_new, post):
    p = pl.program_id(1)
    rows = q_ref.shape[0]

    @pl.when(p == 0)
    def _():
        _flash_init(m_s, l_s, acc_s)
        mask_s[...] = _head_mask(rows, mask_s.shape[1], heads, t_new, False)
        s = lax.dot_general(q_ref[...], kn_ref[...].astype(BF16), NT, preferred_element_type=F32)
        _paged_update(s + _head_mask(rows, kn_ref.shape[0], heads, t_new, True), vn_ref[...], m_s, l_s, acc_s)

    @pl.when(p > 0)
    def _():
        s = lax.dot_general(q_ref[...], kc_ref[...].astype(BF16), NT, preferred_element_type=F32)
        _paged_update(s + mask_s[...], vc_ref[...], m_s, l_s, acc_s)

    @pl.when(p == pl.num_programs(1) - 1)
    def _():
        a = acc_s[...] / l_s[...]
        o = a[:rows // 2] - lam_ref[0] * a[rows // 2:]
        y = o * lax.rsqrt(jnp.mean(o * o, axis=-1, keepdims=True) + EPS) * g_ref[...]
        o_ref[...] = (y * post).astype(o_ref.dtype)


def _new_rows(x, nseq, width):
    x = x.reshape(nseq, -1, width)
    pad = -x.shape[1] % LANES
    return jnp.pad(x, ((0, 0), (0, pad), (0, 0)))


def diff_attention_paged(qa, ka, va, cache_k, cache_v, page_table, lam, subln_g, post, *, heads, t_new):
    nseq, n_pages = page_table.shape
    dv = 2 * A_DH
    rows = 2 * heads * t_new
    q5 = qa.reshape(nseq, t_new, heads, 2, A_DH) * (A_DH ** -0.5)
    zero = jnp.zeros_like(q5[:, :, :, 0])
    qm = jnp.stack([jnp.concatenate([q5[:, :, :, 0], zero], axis=-1),
                    jnp.concatenate([zero, q5[:, :, :, 1]], axis=-1)], axis=1)
    qm = jnp.transpose(qm, (0, 1, 3, 2, 4)).reshape(nseq, rows, dv).astype(BF16)
    kn, vn = _new_rows(ka, nseq, dv), _new_rows(va, nseq, dv)
    page = lambda s, p, pt: (pt[s, jnp.maximum(p - 1, 0)], 0, 0)
    per_seq = lambda s, p, pt: (s, 0, 0)
    keys = PAGE * heads
    out = pl.pallas_call(
        functools.partial(_paged_diff_kernel, heads=heads, t_new=t_new, post=post),
        out_shape=jax.ShapeDtypeStruct((nseq, rows // 2, dv), BF16),
        grid_spec=pltpu.PrefetchScalarGridSpec(
            num_scalar_prefetch=1,
            grid=(nseq, n_pages + 1),
            in_specs=[pl.BlockSpec(memory_space=pltpu.SMEM),
                      pl.BlockSpec((None, rows, dv), per_seq),
                      pl.BlockSpec((None, keys, dv), page), pl.BlockSpec((None, keys, dv), page),
                      pl.BlockSpec((None, kn.shape[1], dv), per_seq), pl.BlockSpec((None, kn.shape[1], dv), per_seq),
                      pl.BlockSpec((1, dv), lambda s, p, pt: (0, 0))],
            out_specs=pl.BlockSpec((None, rows // 2, dv), per_seq),
            scratch_shapes=[pltpu.VMEM((rows, LANES), F32), pltpu.VMEM((rows, LANES), F32), pltpu.VMEM((rows, dv), F32),
                            pltpu.VMEM((rows, keys), F32)]),
        compiler_params=_cparams("parallel", "arbitrary"),
        name="diff_attention_paged",
    )(page_table, lam.reshape(1).astype(F32), qm, cache_k, cache_v, kn, vn, subln_g.reshape(1, dv).astype(F32))
    out = jnp.transpose(out.reshape(nseq, heads, t_new, dv), (0, 2, 1, 3))
    return out.reshape(nseq * t_new, heads * dv)


def _page_suffix_kernel(lf_ref, later_ref, tot_ref):
    a = lax.broadcasted_iota(jnp.int32, (PAGE, PAGE), 0)
    b = lax.broadcasted_iota(jnp.int32, (PAGE, PAGE), 1)
    after = (b > a).astype(F32)
    ones = jnp.ones((PAGE, PAGE), F32)
    for i in range(lf_ref.shape[0]):
        lf = lf_ref[i]
        later_ref[i] = jnp.dot(after, lf, precision=HI, preferred_element_type=F32)
        tot_ref[i] = jnp.dot(ones, lf, precision=HI, preferred_element_type=F32)


def page_suffix_sums(cache_lf):
    n_pool, _, heads = cache_lf.shape
    pb = math.gcd(n_pool, 8)
    blk = pl.BlockSpec((pb, PAGE, heads), lambda i: (i, 0, 0))
    return pl.pallas_call(
        _page_suffix_kernel,
        out_shape=[jax.ShapeDtypeStruct(cache_lf.shape, F32)] * 2,
        grid=(n_pool // pb,),
        in_specs=[blk],
        out_specs=[blk, blk],
        compiler_params=_cparams("parallel"),
        name="page_suffix_sums",
    )(cache_lf)


def _paged_fox_kernel(pt_ref, q_ref, kc_ref, vc_ref, later_ref, tot_ref, kn_ref, vn_ref, lfn_ref, o_ref,
                      m_s, l_s, acc_s, suf_s, mask_s, *, heads, t_new):
    p = pl.program_id(1)
    rows = q_ref.shape[0]

    @pl.when(p == 0)
    def _():
        _flash_init(m_s, l_s, acc_s)
        suf_s[...] = jnp.zeros(suf_s.shape, F32)
        mask_s[...] = _head_mask(rows, mask_s.shape[1], heads, t_new, False)
        nkeys = kn_ref.shape[0]
        a = lax.broadcasted_iota(jnp.int32, (nkeys, nkeys), 0)
        b = lax.broadcasted_iota(jnp.int32, (nkeys, nkeys), 1)
        upto = (((a % heads) == (b % heads)) & ((a // heads) <= (b // heads))).astype(F32)
        ncum = jnp.dot(jnp.broadcast_to(lfn_ref[...], (SUBLANES, nkeys)), upto, precision=HI,
                       preferred_element_type=F32)[0:1, :]
        s = lax.dot_general(q_ref[...], kn_ref[...].astype(BF16), NT, preferred_element_type=F32) - ncum
        _paged_update(s + _head_mask(rows, nkeys, heads, t_new, True), vn_ref[...], m_s, l_s, acc_s)

    @pl.when(p > 0)
    def _():
        s = lax.dot_general(q_ref[...], kc_ref[...].astype(BF16), NT, preferred_element_type=F32)
        s = s + (later_ref[...] + suf_s[...]) + mask_s[...]
        _paged_update(s, vc_ref[...], m_s, l_s, acc_s)
        suf_s[...] = suf_s[...] + tot_ref[...]

    @pl.when(p == pl.num_programs(1) - 1)
    def _():
        o_ref[...] = (acc_s[...] / l_s[...]).astype(o_ref.dtype)


def fox_attention_paged(q, k, v, logf_new, cache_k, cache_v, later, tot, page_table, *, heads, dh, t_new):
    nseq, n_pages = page_table.shape
    rows = heads * t_new
    qm = jnp.transpose((q * (dh ** -0.5)).reshape(nseq, t_new, heads, dh), (0, 2, 1, 3))
    qm = qm.reshape(nseq, rows, dh).astype(BF16)
    kn, vn = _new_rows(k, nseq, dh), _new_rows(v, nseq, dh)
    lfn = _new_rows(logf_new.astype(F32), nseq, 1).reshape(nseq, 1, -1)
    page = lambda s, p, pt: (pt[s, n_pages - jnp.maximum(p, 1)], 0, 0)
    per_seq = lambda s, p, pt: (s, 0, 0)
    keys = PAGE * heads
    nkeys = kn.shape[1]
    out = pl.pallas_call(
        functools.partial(_paged_fox_kernel, heads=heads, t_new=t_new),
        out_shape=jax.ShapeDtypeStruct((nseq, rows, dh), BF16),
        grid_spec=pltpu.PrefetchScalarGridSpec(
            num_scalar_prefetch=1,
            grid=(nseq, n_pages + 1),
            in_specs=[pl.BlockSpec((None, rows, dh), per_seq),
                      pl.BlockSpec((None, keys, dh), page), pl.BlockSpec((None, keys, dh), page),
                      pl.BlockSpec((None, 1, keys), page), pl.BlockSpec((None, 1, keys), page),
                      pl.BlockSpec((None, nkeys, dh), per_seq), pl.BlockSpec((None, nkeys, dh), per_seq),
                      pl.BlockSpec((None, 1, nkeys), per_seq)],
            out_specs=pl.BlockSpec((None, rows, dh), per_seq),
            scratch_shapes=[pltpu.VMEM((rows, LANES), F32), pltpu.VMEM((rows, LANES), F32), pltpu.VMEM((rows, dh), F32),
                            pltpu.VMEM((1, keys), F32), pltpu.VMEM((rows, keys), F32)]),
        compiler_params=_cparams("parallel", "arbitrary"),
        name="fox_attention_paged",
    )(page_table, qm, cache_k, cache_v, later, tot, kn, vn, lfn)
    out = jnp.transpose(out.reshape(nseq, heads, t_new, dh), (0, 2, 1, 3))
    return out.reshape(nseq * t_new, heads * dh)


def _rope_tables(pos):
    rot = A_DH // 4
    half = rot // 2
    inv = jnp.power(ROPE_THETA, -jnp.arange(0, rot, 2, dtype=F32) / rot)
    ang = pos.astype(F32)[:, None] * inv[None, :]
    cos, sin = jnp.cos(ang), jnp.sin(ang)
    n = pos.shape[0]
    ones = jnp.ones((n, A_DH - rot), F32)
    zeros = jnp.zeros((n, A_DH - rot), F32)
    zh = jnp.zeros((n, half), F32)
    c = jnp.concatenate([cos, cos, ones], axis=1)
    sp = jnp.concatenate([-sin, zh, zeros], axis=1)
    sm = jnp.concatenate([zh, sin, zeros], axis=1)
    rep = LANES // A_DH
    return jnp.tile(c, (1, rep)), jnp.tile(sp, (1, rep)), jnp.tile(sm, (1, rep))


def _pad_rows(x, m):
    return x if x.shape[0] == m else jnp.pad(x, ((0, m - x.shape[0]), (0, 0)))


def _even_layer(h, pos, n_tok, attend, conv_buf, rec0, gdn_shape, w, lam, lam_init):
    heads_b = w["heads_b"]
    xn = rms_norm([h], w["ln_mix"])
    tabs = tuple(_pad_rows(t, h.shape[0]) for t in _rope_tables(pos))
    qa = matmul(xn, w["w_qa"], mode="rope", extra=tabs)
    ka = matmul(xn, w["w_ka"], mode="rope", extra=tabs)
    va = matmul(xn, w["w_va"])
    qkv = matmul(xn, w["w_qkv"])
    z = matmul(xn, w["w_z"])
    ab = matmul(xn, w["w_ab"])
    att = attend(qa[:n_tok], ka[:n_tok], va[:n_tok], lam, w["subln_g"], 1.0 - lam_init)
    bsz, seq = gdn_shape
    seq_p = -(-seq // DN_CHUNK) * DN_CHUNK

    def seq_pad(x):
        x = x[:n_tok].reshape(bsz, seq, x.shape[-1])
        if seq_p != seq:
            x = jnp.pad(x, ((0, 0), (0, seq_p - seq), (0, 0)))
        return x

    gate = lambda x: jnp.transpose(seq_pad(x), (0, 2, 1))[..., None]
    buf8 = jnp.pad(conv_buf.astype(F32), ((0, 0), (SUBLANES - conv_buf.shape[1], 0), (0, 0)))
    dn, rec = gated_deltanet(
        seq_pad(qkv).reshape(bsz * seq_p, -1), seq_pad(z).reshape(bsz * seq_p, -1),
        gate(ab[:, :heads_b]), gate(ab[:, heads_b:2 * heads_b]), buf8, rec0.astype(F32),
        w["conv_w"], w["a_log"], w["dt_bias"], w["dn_norm_g"],
        bsz=bsz, seq=seq_p, heads=heads_b, valid=min(seq, DN_CHUNK))
    dn = dn.reshape(bsz, seq_p, -1)[:, :seq].reshape(n_tok, -1)
    xp = jnp.concatenate([conv_buf.astype(F32), qkv[:n_tok].reshape(bsz, seq, -1)], axis=1)
    conv_new = xp[:, seq:]
    m = h.shape[0]
    h = matmul(_pad_rows(att, m), w["w_out_a"], mode="resid", extra=(h,))
    h = matmul(_pad_rows(dn, m), w["w_out_b"], mode="resid", extra=(h,))
    return h, ka[:n_tok], va[:n_tok], conv_new, rec


def _odd_layer(h_parts, n_tok, attend, w):
    h, xn = rms_norm(h_parts, w["ln_mix"], emit_sum=True)
    q = matmul(xn, w["w_q"])
    k = matmul(xn, w["w_k"])
    v = matmul(xn, w["w_v"])
    logf = matmul(xn, w["w_f"], mode="logsig", extra=(w["b_f"],))[:, :w["heads_c"]]
    ctx = attend(q[:n_tok], k[:n_tok], v[:n_tok], logf[:n_tok])
    h = matmul(_pad_rows(ctx, h.shape[0]), w["w_out"], mode="resid", extra=(h,))
    return h, k[:n_tok], v[:n_tok], logf[:n_tok]


def kernel(x_prompt, x_sample, cache_a_k, cache_a_v, state_b_conv, state_b_rec, cache_c_k, cache_c_v, cache_c_logf, page_table, ln_mix, ln_ffn, ln_out, w_in_even, w_out_even, lam_q1, lam_k1, lam_q2, lam_k2, subln_g, conv_w, a_log, dt_bias, dn_norm_g, w_in_odd, b_forget, w_out_odd, peer_wq, peer_keys, peer_u, peer_v):
    bsz, seq, d = x_prompt.shape
    dbsz, dseq, _ = x_sample.shape
    depth = ln_mix.shape[0]
    heads_a, a_dv = cache_a_k.shape[3], cache_a_k.shape[4]
    a_w = heads_a * a_dv
    heads_b, dk = state_b_rec.shape[2], state_b_rec.shape[3]
    heads_c, c_dh = cache_c_k.shape[3], cache_c_k.shape[4]
    c_w = heads_c * c_dh
    nk, dq = peer_keys.shape[2], peer_keys.shape[3]
    assert nk == dq and a_dv == 2 * A_DH
    peer_heads = peer_wq.shape[-1] // (2 * dq)
    past_len = page_table.shape[1] * PAGE
    n_p, n_s = bsz * seq, dbsz * dseq
    m_s = -(-n_s // LANES) * LANES

    hp = x_prompt.reshape(n_p, d).astype(F32)
    hs = _pad_rows(x_sample.reshape(n_s, d).astype(F32), m_s)
    pos_p = jnp.tile(jnp.arange(seq), bsz)
    pos_s = jnp.tile(past_len + jnp.arange(dseq), dbsz)
    bf = lambda x: x.astype(BF16)
    lane_pad = lambda x: jnp.pad(x, ((0, 0), (0, LANES - x.shape[1])))

    outs_p = {k: [] for k in ("ak", "av", "bc", "br", "ck", "cv", "cf")}
    outs_s = {k: [] for k in ("ak", "av", "bc", "br", "ck", "cv", "cf")}
    peer_p = peer_s = None
    for li in range(depth):
        if li % 2 == 0:
            e = li // 2
            lam_init = 0.8 - 0.6 * math.exp(-0.3 * li)
            lam = (jnp.exp(jnp.sum(lam_q1[e] * lam_k1[e]).astype(F32))
                   - jnp.exp(jnp.sum(lam_q2[e] * lam_k2[e]).astype(F32)) + lam_init)
            wi, wo = w_in_even[e], w_out_even[e]
            o1, o2, o3 = a_w, 2 * a_w, 3 * a_w
            o4 = o3 + 3 * heads_b * dk
            o5 = o4 + heads_b * dk
            w = dict(heads_b=heads_b, ln_mix=ln_mix[li], subln_g=subln_g[e],
                     w_qa=bf(wi[:, :o1]), w_ka=bf(wi[:, o1:o2]), w_va=bf(wi[:, o2:o3]), w_qkv=bf(wi[:, o3:o4]),
                     w_z=bf(wi[:, o4:o5]), w_ab=bf(lane_pad(wi[:, o5:])),
                     w_out_a=bf(wo[:a_w]), w_out_b=bf(wo[a_w:]),
                     conv_w=conv_w[e].astype(F32), a_log=a_log[e], dt_bias=dt_bias[e], dn_norm_g=dn_norm_g[e])
            if peer_p is not None:
                hp, hs = hp + peer_p, hs + peer_s
            attend_p = functools.partial(diff_attention_prompt, bsz=bsz, seq=seq, heads=heads_a)
            hp, ka, va, cb, rs = _even_layer(
                hp, pos_p, n_p, attend_p, jnp.zeros((bsz, CONV_TAPS - 1, 3 * heads_b * dk), F32),
                jnp.zeros((bsz, heads_b, dk, dk), F32), (bsz, seq), w, lam, lam_init)
            for key, val in zip(("ak", "av", "bc", "br"), (ka, va, cb, rs)):
                outs_p[key].append(val)
            pool_k = cache_a_k[e].reshape(-1, PAGE * heads_a, a_dv)
            pool_v = cache_a_v[e].reshape(-1, PAGE * heads_a, a_dv)
            attend_s = lambda q, k, v, lam_, g_, post_: diff_attention_paged(
                q, k, v, pool_k, pool_v, page_table, lam_, g_, post_, heads=heads_a, t_new=dseq)
            hs, ka, va, cb, rs = _even_layer(
                hs, pos_s, n_s, attend_s, state_b_conv[e], state_b_rec[e], (dbsz, dseq), w, lam, lam_init)
            for key, val in zip(("ak", "av", "bc", "br"), (ka, va, cb, rs)):
                outs_s[key].append(val)
            parts_p, parts_s = [hp], [hs]
        else:
            o = li // 2
            wi = w_in_odd[o]
            w = dict(heads_c=heads_c, ln_mix=ln_mix[li], w_q=bf(wi[:, :c_w]), w_k=bf(wi[:, c_w:2 * c_w]),
                     w_v=bf(wi[:, 2 * c_w:3 * c_w]), w_f=bf(lane_pad(wi[:, 3 * c_w:])),
                     b_f=lane_pad(b_forget[o].reshape(1, heads_c).astype(F32)), w_out=bf(w_out_odd[o]))

            def attend_p(q, k, v, logf):
                f = cumsum_time(lane_pad(logf).reshape(bsz, seq, LANES))[:, :, :heads_c]
                f_row = jnp.transpose(f, (0, 2, 1))[:, :, None, :]
                return fox_attention_prompt(q, k, v, f_row, bsz=bsz, seq=seq, heads=heads_c, dh=c_dh)

            later, tot = page_suffix_sums(cache_c_logf[o].astype(F32))
            flat = lambda x: x.reshape(-1, 1, PAGE * heads_c)
            pool_k = cache_c_k[o].reshape(-1, PAGE * heads_c, c_dh)
            pool_v = cache_c_v[o].reshape(-1, PAGE * heads_c, c_dh)
            attend_s = lambda q, k, v, logf: fox_attention_paged(
                q, k, v, logf, pool_k, pool_v, flat(later), flat(tot), page_table,
                heads=heads_c, dh=c_dh, t_new=dseq)
            parts_p = [hp] if peer_p is None else [hp, peer_p]
            parts_s = [hs] if peer_s is None else [hs, peer_s]
            hp, kc, vc, lf = _odd_layer(parts_p, n_p, attend_p, w)
            for key, val in zip(("ck", "cv", "cf"), (kc, vc, lf)):
                outs_p[key].append(val)
            hs, kc, vc, lf = _odd_layer(parts_s, n_s, attend_s, w)
            for key, val in zip(("ck", "cv", "cf"), (kc, vc, lf)):
                outs_s[key].append(val)
            parts_p, parts_s = [hp], [hs]
        wq_t = bf(peer_wq[li].T)
        keys = bf(peer_keys[li])
        u_tab, v_tab = bf(peer_u[li]), bf(peer_v[li])
        peer_p = peer_layer(parts_p, ln_ffn[li], wq_t, keys, u_tab, v_tab, heads=peer_heads, nk=nk)
        peer_s = peer_layer(parts_s, ln_ffn[li], wq_t, keys, u_tab, v_tab, heads=peer_heads, nk=nk)

    y_p = rms_norm([hp, peer_p], ln_out, out_dtype=F32).reshape(bsz, seq, d)
    y_s = rms_norm([hs, peer_s], ln_out, out_dtype=F32)[:n_s].reshape(dbsz, dseq, d)

    def stack(vals, *shape):
        return jnp.stack([v.reshape(*shape) for v in vals])

    def group(outs, b, t):
        return (stack(outs["ak"], b, t, heads_a, a_dv), stack(outs["av"], b, t, heads_a, a_dv),
                stack(outs["bc"], b, CONV_TAPS - 1, 3 * heads_b * dk), stack(outs["br"], b, heads_b, dk, dk),
                stack(outs["ck"], b, t, heads_c, c_dh), stack(outs["cv"], b, t, heads_c, c_dh),
                stack(outs["cf"], b, t, heads_c))

    return (y_p, y_s) + group(outs_p, bsz, seq) + group(outs_s, dbsz, dseq)
```

```python
import functools
import math

import jax
import jax.numpy as jnp
from jax import lax
from jax.experimental import pallas as pl
from jax.experimental.pallas import tpu as pltpu

F32 = jnp.float32
BF16 = jnp.bfloat16
HI = lax.Precision.HIGHEST

EPS = 1e-6
A_DH = 64
ROPE_THETA = 500000.0
CONV_TAPS = 4
DN_CHUNK = 64
GDN_HEADS_PER_STEP = 16
FOX_HEADS_PER_STEP = 2
PEER_TOPK = 16
PEER_TOKEN_SUB = 256
PAGE = 128
LANES = 128
SUBLANES = 8
VMEM_LIMIT = 56 * 1024 * 1024

NT = (((1,), (1,)), ((), ()))
TN = (((0,), (0,)), ((), ()))


def _cparams(*sem):
    return pltpu.CompilerParams(dimension_semantics=sem, vmem_limit_bytes=VMEM_LIMIT)


def _tile(n, pref):
    if n <= pref:
        return n
    t = pref
    while n % t:
        t //= 2
    return t


def _dot1(a, b, dims=None):
    a, b = a.astype(BF16), b.astype(BF16)
    if dims is None:
        return jnp.dot(a, b, preferred_element_type=F32)
    return lax.dot_general(a, b, dims, preferred_element_type=F32)


def _dot3(a, b, dims=None):
    ah, bh = a.astype(BF16), b.astype(BF16)
    al, bl = (a - ah.astype(F32)).astype(BF16), (b - bh.astype(F32)).astype(BF16)
    return _dot1(ah, bh, dims) + _dot1(ah, bl, dims) + _dot1(al, bh, dims)


def _rms_kernel(*refs, n_in, emit_sum, transpose_out):
    x = refs[0][...]
    for r in refs[1:n_in]:
        x = x + r[...]
    g = refs[n_in][...]
    outs = refs[n_in + 1:]
    y = x * lax.rsqrt(jnp.mean(x * x, axis=-1, keepdims=True) + EPS) * g
    k = 0
    if emit_sum:
        outs[0][...] = x
        k = 1
    if transpose_out:
        outs[k][...] = y.T.astype(outs[k].dtype)
    else:
        outs[k][...] = y.astype(outs[k].dtype)


def rms_norm(xs, g, *, out_dtype=BF16, emit_sum=False, transpose_out=False):
    m, d = xs[0].shape
    tm = _tile(m, 256)
    row = pl.BlockSpec((tm, d), lambda i: (i, 0))
    out_shape, out_specs = [], []
    if emit_sum:
        out_shape.append(jax.ShapeDtypeStruct((m, d), F32))
        out_specs.append(row)
    if transpose_out:
        out_shape.append(jax.ShapeDtypeStruct((d, m), out_dtype))
        out_specs.append(pl.BlockSpec((d, tm), lambda i: (0, i)))
    else:
        out_shape.append(jax.ShapeDtypeStruct((m, d), out_dtype))
        out_specs.append(row)
    res = pl.pallas_call(
        functools.partial(_rms_kernel, n_in=len(xs), emit_sum=emit_sum, transpose_out=transpose_out),
        out_shape=out_shape,
        grid=(m // tm,),
        in_specs=[row] * len(xs) + [pl.BlockSpec((1, d), lambda i: (0, 0))],
        out_specs=out_specs,
        compiler_params=_cparams("parallel"),
        name="rms_norm",
    )(*xs, g.reshape(1, d).astype(F32))
    return res if emit_sum else res[0]


def _mm_kernel(*refs, mode, nk):
    a_ref, b_ref = refs[0], refs[1]
    o_ref = refs[-1]
    acc = jnp.dot(a_ref[...], b_ref[...], preferred_element_type=F32)
    if mode == "plain":
        o_ref[...] = acc
    elif mode == "resid":
        o_ref[...] = acc + refs[2][...]
    elif mode == "rope":
        c, sp, sm = refs[2][...], refs[3][...], refs[4][...]
        for g in range(acc.shape[1] // LANES):
            x = acc[:, g * LANES:(g + 1) * LANES]
            o_ref[:, g * LANES:(g + 1) * LANES] = (
                x * c + pltpu.roll(x, LANES - 8, axis=1) * sp + pltpu.roll(x, 8, axis=1) * sm)
    elif mode == "logsig":
        x = acc + refs[2][...]
        o_ref[...] = jnp.minimum(x, 0.0) - jnp.log1p(jnp.exp(-jnp.abs(x)))
    elif mode == "keys":
        keys_ref = refs[2]
        for c in range(2):
            q = acc[c * nk:(c + 1) * nk, :].astype(BF16)
            o_ref[c * nk:(c + 1) * nk, :] = jnp.dot(keys_ref[c], q, preferred_element_type=F32)
    else:
        raise ValueError(mode)


def matmul(a, b, *, mode="plain", extra=(), tm=512, tn=512, nk=0):
    m, k = a.shape
    _, n = b.shape
    tm = _tile(m, tm)
    tn = _tile(n, tn)
    in_specs = [pl.BlockSpec((tm, k), lambda i, j: (i, 0)),
                pl.BlockSpec((k, tn), lambda i, j: (0, j))]
    if mode == "resid":
        in_specs.append(pl.BlockSpec((tm, tn), lambda i, j: (i, j)))
    elif mode == "rope":
        in_specs += [pl.BlockSpec((tm, LANES), lambda i, j: (i, 0))] * 3
    elif mode == "logsig":
        in_specs.append(pl.BlockSpec((1, tn), lambda i, j: (0, j)))
    elif mode == "keys":
        in_specs.append(pl.BlockSpec(extra[0].shape, lambda i, j: (0, 0, 0)))
    return pl.pallas_call(
        functools.partial(_mm_kernel, mode=mode, nk=nk),
        out_shape=jax.ShapeDtypeStruct((m, n), F32),
        grid=(m // tm, n // tn),
        in_specs=in_specs,
        out_specs=pl.BlockSpec((tm, tn), lambda i, j: (i, j)),
        compiler_params=_cparams("parallel", "parallel"),
        name="matmul_" + mode,
    )(a, b, *extra)


def _flash_init(m_s, l_s, acc_s):
    m_s[...] = jnp.full(m_s.shape, -jnp.inf, F32)
    l_s[...] = jnp.zeros(l_s.shape, F32)
    acc_s[...] = jnp.zeros(acc_s.shape, F32)


def _lane_tile(x, width):
    return x if width == LANES else jnp.tile(x, (1, width // LANES))


def _flash_update(ss, vs, m_s, l_s, acc_s):
    n = range(len(ss))
    width = ss[0].shape[1]
    m_prev = [m_s[c] for c in n]
    m_new = [jnp.maximum(m_prev[c], jnp.max(ss[c], axis=1, keepdims=True)) for c in n]
    p = [jnp.exp(ss[c] - _lane_tile(m_new[c], width)) for c in n]
    pv = [jnp.dot(p[c].astype(BF16), vs[c], preferred_element_type=F32) for c in n]
    for c in n:
        alpha = jnp.exp(m_prev[c] - m_new[c])
        l_s[c] = alpha * l_s[c] + jnp.sum(p[c], axis=1, keepdims=True)
        acc_s[c] = alpha * acc_s[c] + pv[c]
        m_s[c] = m_new[c]


def _lower_triangle(tq, tk):
    return lax.broadcasted_iota(jnp.int32, (tq, tk), 1) <= lax.broadcasted_iota(jnp.int32, (tq, tk), 0)


def _diff_flash_kernel(lam_ref, q_ref, k_ref, v_ref, g_ref, o_ref, m_s, l_s, acc_s, *, tq, tk, post):
    i, j = pl.program_id(2), pl.program_id(3)

    @pl.when(j == 0)
    def _():
        _flash_init(m_s, l_s, acc_s)

    def step(masked):
        q = (q_ref[...] * (A_DH ** -0.5)).astype(BF16)
        k = k_ref[...].astype(BF16)
        v = v_ref[...].astype(BF16)
        ss = [lax.dot_general(q[:, c * A_DH:(c + 1) * A_DH], k[:, c * A_DH:(c + 1) * A_DH], NT,
                              preferred_element_type=F32) for c in range(2)]
        if masked:
            ss = [jnp.where(_lower_triangle(tq, tk), s, -jnp.inf) for s in ss]
        _flash_update(ss, [v, v], m_s, l_s, acc_s)

    pl.when(j < i)(functools.partial(step, False))
    pl.when(j == i)(functools.partial(step, True))

    @pl.when(j == pl.num_programs(3) - 1)
    def _():
        o = acc_s[0] / l_s[0] - lam_ref[0] * (acc_s[1] / l_s[1])
        y = o * lax.rsqrt(jnp.mean(o * o, axis=-1, keepdims=True) + EPS) * g_ref[...]
        o_ref[...] = (y * post).astype(o_ref.dtype)


def diff_attention_prompt(qa, ka, va, lam, subln_g, post, *, bsz, seq, heads):
    dv = 2 * A_DH
    tq = tk = _tile(seq, 512)
    nq = nk = seq // tq
    kv_map = lambda b, h, i, j: (b * nk + jnp.minimum(j, i), h)
    return pl.pallas_call(
        functools.partial(_diff_flash_kernel, tq=tq, tk=tk, post=post),
        out_shape=jax.ShapeDtypeStruct((bsz * seq, heads * dv), BF16),
        grid=(bsz, heads, nq, nk),
        in_specs=[pl.BlockSpec(memory_space=pltpu.SMEM),
                  pl.BlockSpec((tq, dv), lambda b, h, i, j: (b * nq + i, h)),
                  pl.BlockSpec((tk, dv), kv_map),
                  pl.BlockSpec((tk, dv), kv_map),
                  pl.BlockSpec((1, dv), lambda b, h, i, j: (0, 0))],
        out_specs=pl.BlockSpec((tq, dv), lambda b, h, i, j: (b * nq + i, h)),
        scratch_shapes=[pltpu.VMEM((2, tq, LANES), F32), pltpu.VMEM((2, tq, LANES), F32),
                        pltpu.VMEM((2, tq, dv), F32)],
        compiler_params=_cparams("parallel", "parallel", "parallel", "arbitrary"),
        name="diff_attention_prompt",
    )(lam.reshape(1).astype(F32), qa, ka, va, subln_g.reshape(1, dv).astype(F32))


def _fox_flash_kernel(q_ref, k_ref, v_ref, fk_ref, o_ref, m_s, l_s, acc_s, *, tq, tk, dh, hb, scale):
    i, j = pl.program_id(2), pl.program_id(3)

    @pl.when(j == 0)
    def _():
        _flash_init(m_s, l_s, acc_s)

    def step(masked):
        q = (q_ref[...] * scale).astype(BF16)
        k = k_ref[...].astype(BF16)
        v = v_ref[...].astype(BF16)
        cols = [slice(h * dh, (h + 1) * dh) for h in range(hb)]
        ss = [lax.dot_general(q[:, c], k[:, c], NT, preferred_element_type=F32) - fk_ref[h]
              for h, c in enumerate(cols)]
        if masked:
            ss = [jnp.where(_lower_triangle(tq, tk), s, -jnp.inf) for s in ss]
        _flash_update(ss, [v[:, c] for c in cols], m_s, l_s, acc_s)

    pl.when(j < i)(functools.partial(step, False))
    pl.when(j == i)(functools.partial(step, True))

    @pl.when(j == pl.num_programs(3) - 1)
    def _():
        for h in range(hb):
            o_ref[:, h * dh:(h + 1) * dh] = (acc_s[h] / l_s[h]).astype(o_ref.dtype)


def fox_attention_prompt(q, k, v, f_row, *, bsz, seq, heads, dh):
    tq = tk = _tile(seq, 512)
    nq = nk = seq // tq
    hb = math.gcd(heads, FOX_HEADS_PER_STEP)
    w = hb * dh
    kv_map = lambda b, g, i, j: (b * nk + jnp.minimum(j, i), g)
    return pl.pallas_call(
        functools.partial(_fox_flash_kernel, tq=tq, tk=tk, dh=dh, hb=hb, scale=dh ** -0.5),
        out_shape=jax.ShapeDtypeStruct((bsz * seq, heads * dh), BF16),
        grid=(bsz, heads // hb, nq, nk),
        in_specs=[pl.BlockSpec((tq, w), lambda b, g, i, j: (b * nq + i, g)),
                  pl.BlockSpec((tk, w), kv_map),
                  pl.BlockSpec((tk, w), kv_map),
                  pl.BlockSpec((None, hb, 1, tk), lambda b, g, i, j: (b, g, 0, jnp.minimum(j, i)))],
        out_specs=pl.BlockSpec((tq, w), lambda b, g, i, j: (b * nq + i, g)),
        scratch_shapes=[pltpu.VMEM((hb, tq, LANES), F32), pltpu.VMEM((hb, tq, LANES), F32),
                        pltpu.VMEM((hb, tq, dh), F32)],
        compiler_params=_cparams("parallel", "parallel", "parallel", "arbitrary"),
        name="fox_attention_prompt",
    )(q, k, v, f_row)


def _cumsum_kernel(x_ref, o_ref, carry_s, *, tc):
    @pl.when(pl.program_id(1) == 0)
    def _():
        carry_s[...] = jnp.zeros(carry_s.shape, F32)

    r = lax.broadcasted_iota(jnp.int32, (tc, tc), 0)
    c = lax.broadcasted_iota(jnp.int32, (tc, tc), 1)
    tri = (c <= r).astype(F32)
    y = jnp.dot(tri, x_ref[...], precision=HI, preferred_element_type=F32) + carry_s[...]
    o_ref[...] = y
    carry_s[...] = y[tc - 1:tc, :]


def cumsum_time(x):
    bsz, seq, w = x.shape
    tc = _tile(seq, 256)
    return pl.pallas_call(
        functools.partial(_cumsum_kernel, tc=tc),
        out_shape=jax.ShapeDtypeStruct(x.shape, F32),
        grid=(bsz, seq // tc),
        in_specs=[pl.BlockSpec((None, tc, w), lambda b, i: (b, i, 0))],
        out_specs=pl.BlockSpec((None, tc, w), lambda b, i: (b, i, 0)),
        scratch_shapes=[pltpu.VMEM((1, w), F32)],
        compiler_params=_cparams("parallel", "arbitrary"),
        name="cumsum_time",
    )(x)


def _each(f, *lists):
    return [f(*args) for args in zip(*lists)]


def _unit_lower_inverse(ns, size):
    ri = lax.broadcasted_iota(jnp.int32, (size, size), 0)
    ci = lax.broadcasted_iota(jnp.int32, (size, size), 1)
    eye = (ri == ci).astype(F32)
    bs = min(size, 16)
    nd = [jnp.where((ri // bs) == (ci // bs), n, 0.0) for n in ns]
    x, pw, p = [eye - m for m in nd], _each(_dot3, nd, nd), 2
    while p < bs:
        x = _each(lambda a, b: _dot3(a, eye + b), x, pw)
        if 2 * p < bs:
            pw = _each(_dot3, pw, pw)
        p *= 2
    if size == bs:
        return x
    nb = size // bs
    blk = _each(lambda a, n, m: _dot3(a, n - m), x, ns, nd)
    y = [eye - m for m in blk]
    if nb > 2:
        pw, p = _each(_dot3, blk, blk), 2
        while p < nb:
            y = _each(lambda a, b: _dot3(a, eye + b), y, pw)
            if 2 * p < nb:
                pw = _each(_dot3, pw, pw)
            p *= 2
    return _each(_dot3, y, x)


def _gdn_kernel(nea_ref, dtb_ref, xq_ref, xk_ref, xv_ref, z_ref, a_ref, b_ref,
                cq_ref, ck_ref, cv_ref, wq_ref, wk_ref, wv_ref, s0_ref, ng_ref,
                o_ref, sfin_ref, prev_s, state_s, *, chunk, valid, dk, hb):
    head0, c = pl.program_id(1) * hb, pl.program_id(2)
    halo = SUBLANES

    @pl.when(c == 0)
    def _():
        prev_s[0] = cq_ref[...]
        prev_s[1] = ck_ref[...]
        prev_s[2] = cv_ref[...]
        state_s[...] = s0_ref[...]

    def conv_act(idx, x_ref, w_ref):
        x = x_ref[...]
        w = w_ref[...]
        xx = jnp.concatenate([prev_s[idx], x], axis=0)
        y = x * w[CONV_TAPS - 1:CONV_TAPS, :]
        for s in range(1, CONV_TAPS):
            y = y + pltpu.roll(xx, s, axis=0)[halo:halo + chunk] * w[CONV_TAPS - 1 - s:CONV_TAPS - s, :]
        prev_s[idx] = x[chunk - halo:chunk]
        return y / (1.0 + jnp.exp(-y))

    aq_all, ak_all, v_all = conv_act(0, xq_ref, wq_ref), conv_act(1, xk_ref, wk_ref), conv_act(2, xv_ref, wv_ref)
    zz = z_ref[...]
    gate_out = zz / (1.0 + jnp.exp(-zz))
    ri = lax.broadcasted_iota(jnp.int32, (chunk, chunk), 0)
    ci = lax.broadcasted_iota(jnp.int32, (chunk, chunk), 1)
    incl = ci <= ri
    row_b = lax.broadcasted_iota(jnp.int32, (chunk, dk), 0)

    def prepare(i):
        sl = slice(i * dk, (i + 1) * dk)
        aq, ak, v = aq_all[:, sl], ak_all[:, sl], v_all[:, sl]
        q = aq * lax.rsqrt(jnp.sum(aq * aq, axis=-1, keepdims=True) + EPS) * (dk ** -0.5)
        k = ak * lax.rsqrt(jnp.sum(ak * ak, axis=-1, keepdims=True) + EPS)
        x = a_ref[i] + dtb_ref[head0 + i]
        g = nea_ref[head0 + i] * (jnp.maximum(x, 0.0) + jnp.log1p(jnp.exp(-jnp.abs(x))))
        beta = 1.0 / (1.0 + jnp.exp(-b_ref[i]))
        if valid < chunk:
            live = lax.broadcasted_iota(jnp.int32, (chunk, 1), 0) < valid
            g = jnp.where(live, g, 0.0)
            beta = jnp.where(live, beta, 0.0)
            k = jnp.where(live, k, 0.0)
        gc_b = jnp.broadcast_to(g, (chunk, dk))
        s = 1
        while s < chunk:
            gc_b = gc_b + jnp.where(row_b >= s, pltpu.roll(gc_b, s, axis=0), 0.0)
            s *= 2
        gc_row = jnp.concatenate([gc_b] * (LANES // chunk), axis=0).T[:chunk, :chunk]
        decay = jnp.exp(jnp.where(incl, gc_b[:, :chunk] - gc_row, -jnp.inf))
        return q, k, v * beta, k * beta, gc_b, decay

    heads = range(hb)
    q, k, vb, kb, gc_b, decay = zip(*[prepare(i) for i in heads])
    lower = _each(lambda a, b, d: jnp.where(ci < ri, _dot3(a, b, NT) * d, 0.0), kb, k, decay)
    intra = _each(lambda a, b, d: jnp.where(incl, _dot1(a, b, NT) * d, 0.0), q, k, decay)
    inv = _unit_lower_inverse(lower, chunk)
    egc = [jnp.exp(x) for x in gc_b]
    uw = _each(lambda t, a, b, e: _dot3(t, jnp.concatenate([a, b * e], axis=1)), inv, vb, kb, egc)
    state = [state_s[i] for i in heads]
    new_v = _each(lambda x, s: x[:, :dk] - _dot3(x[:, dk:], s), uw, state)
    o = _each(lambda a, e, s, t, nv: _dot1(a * e, s) + _dot1(t, nv), q, egc, state, intra, new_v)
    g_last = [x[chunk - 1:chunk, :] for x in gc_b]
    state = _each(lambda s, gl, a, gc, nv: s * jnp.exp(gl) + _dot3(a * jnp.exp(gl - gc), nv, TN),
                  state, g_last, k, gc_b, new_v)
    for i in heads:
        sl = slice(i * dk, (i + 1) * dk)
        state_s[i] = state[i]
        y = o[i] * lax.rsqrt(jnp.mean(o[i] * o[i], axis=-1, keepdims=True) + EPS) * ng_ref[...]
        o_ref[:, sl] = (y * gate_out[:, sl]).astype(o_ref.dtype)

    @pl.when(c == pl.num_programs(2) - 1)
    def _():
        sfin_ref[...] = state_s[...]


def gated_deltanet(qkv, z, a, b, conv_buf8, s0, conv_w, a_log, dt_bias, norm_g, *, bsz, seq, heads, valid):
    dk = LANES
    chunk = DN_CHUNK
    assert LANES % chunk == 0
    nch = seq // chunk
    assert nch == 1 or valid == chunk
    hb = math.gcd(heads, GDN_HEADS_PER_STEP)
    ng = heads // hb
    w = hb * dk
    col = lambda off: (lambda bb, g, c: (bb * nch + c, off + g))
    buf = lambda off: (lambda bb, g, c: (bb, 0, off + g))
    wsp = lambda off: (lambda bb, g, c: (0, off + g))
    gate = pl.BlockSpec((None, hb, chunk, 1), lambda bb, g, c: (bb, g, c, 0))
    st = pl.BlockSpec((None, hb, dk, dk), lambda bb, g, c: (bb, g, 0, 0))
    smem = pl.BlockSpec(memory_space=pltpu.SMEM)
    return pl.pallas_call(
        functools.partial(_gdn_kernel, chunk=chunk, valid=valid, dk=dk, hb=hb),
        out_shape=[jax.ShapeDtypeStruct((bsz * seq, heads * dk), BF16),
                   jax.ShapeDtypeStruct((bsz, heads, dk, dk), F32)],
        grid=(bsz, ng, nch),
        in_specs=[smem, smem,
                  pl.BlockSpec((chunk, w), col(0)), pl.BlockSpec((chunk, w), col(ng)),
                  pl.BlockSpec((chunk, w), col(2 * ng)), pl.BlockSpec((chunk, w), col(0)),
                  gate, gate,
                  pl.BlockSpec((None, SUBLANES, w), buf(0)), pl.BlockSpec((None, SUBLANES, w), buf(ng)),
                  pl.BlockSpec((None, SUBLANES, w), buf(2 * ng)),
                  pl.BlockSpec((CONV_TAPS, w), wsp(0)), pl.BlockSpec((CONV_TAPS, w), wsp(ng)),
                  pl.BlockSpec((CONV_TAPS, w), wsp(2 * ng)),
                  st, pl.BlockSpec((1, dk), lambda bb, g, c: (0, 0))],
        out_specs=[pl.BlockSpec((chunk, w), col(0)), st],
        scratch_shapes=[pltpu.VMEM((3, SUBLANES, w), F32), pltpu.VMEM((hb, dk, dk), F32)],
        compiler_params=_cparams("parallel", "parallel", "arbitrary"),
        name="gated_deltanet",
    )(-jnp.exp(a_log.astype(F32)), dt_bias.astype(F32), qkv, qkv, qkv, z, a, b,
      conv_buf8, conv_buf8, conv_buf8, conv_w, conv_w, conv_w, s0, norm_g.reshape(1, dk).astype(F32))


def _extract_top(s, count):
    rows = lax.broadcasted_iota(jnp.int32, s.shape, 0)
    work = s
    vals = []
    for _ in range(count):
        m = jnp.max(work, axis=0, keepdims=True)
        idx = jnp.min(jnp.where(work == m, rows, s.shape[0]), axis=0, keepdims=True)
        work = jnp.where(rows == idx, -jnp.inf, work)
        vals.append(m)
    return vals, work < s


def _peer_select_kernel(s_ref, a1_ref, b2_ref, tau_ref, *, nk):
    s1 = s_ref[0:nk, :]
    s2 = s_ref[nk:2 * nk, :]
    tl = s1.shape[1]
    v1, mem1 = _extract_top(s1, PEER_TOPK)
    v2, mem2 = _extract_top(s2, PEER_TOPK)
    half = PEER_TOPK // 2
    sub = lax.broadcasted_iota(jnp.int32, (PEER_TOPK, tl), 0)
    v2_all = jnp.zeros((PEER_TOPK, tl), F32)
    v1_hi = jnp.zeros((half, tl), F32)
    for t in range(PEER_TOPK):
        v2_all = jnp.where(sub == t, v2[t], v2_all)
    sub_hi = lax.broadcasted_iota(jnp.int32, (half, tl), 0)
    for t in range(half):
        v1_hi = jnp.where(sub_hi == t, v1[half + t], v1_hi)
    cand = jnp.concatenate([v1[t] + v2_all for t in range(half)] + [v1_hi + v2[0]], axis=0)
    top, _ = _extract_top(cand, PEER_TOPK)
    z = jnp.ones((1, tl), F32)
    for t in range(1, PEER_TOPK):
        z = z + jnp.exp(top[t] - top[0])
    a1_ref[...] = jnp.where(mem1, jnp.exp(s1 - v1[0]), 0.0) / z
    b2_ref[...] = jnp.where(mem2, jnp.exp(s2 - v2[0]), 0.0)
    tau_ref[...] = top[PEER_TOPK - 1]


def peer_select(s_t, *, heads, nk):
    n = s_t.shape[1]
    tl = _tile(n, 256)
    fac = pl.BlockSpec((nk, tl), lambda i, h: (h, i))
    return pl.pallas_call(
        functools.partial(_peer_select_kernel, nk=nk),
        out_shape=[jax.ShapeDtypeStruct((heads * nk, n), F32), jax.ShapeDtypeStruct((heads * nk, n), F32),
                   jax.ShapeDtypeStruct((heads, 1, n), F32)],
        grid=(n // tl, heads),
        in_specs=[pl.BlockSpec((2 * nk, tl), lambda i, h: (h, i))],
        out_specs=[fac, fac, pl.BlockSpec((None, 1, tl), lambda i, h: (h, 0, i))],
        compiler_params=_cparams("parallel", "parallel"),
        name="peer_select",
    )(s_t)


def _peer_expert_kernel(xt_ref, s_ref, a1_ref, b2_ref, tau_ref, u_ref, v_ref, o_ref, *, heads, nk, te):
    e = pl.program_id(1)

    @pl.when(e == 0)
    def _():
        o_ref[...] = jnp.zeros(o_ref.shape, F32)

    tm = xt_ref.shape[1]
    tsub = PEER_TOKEN_SUB if tm % PEER_TOKEN_SUB == 0 else tm
    subs = [slice(t * tsub, (t + 1) * tsub) for t in range(tm // tsub)]
    def gate_of(c):
        parts = []
        for a in range(te // nk):
            i1 = e * (te // nk) + a
            w = jnp.zeros((nk, tsub), F32)
            for h in range(heads):
                s1 = s_ref[pl.ds(h * 2 * nk + i1, 1), c]
                a1 = a1_ref[pl.ds(h * nk + i1, 1), c]
                s2 = s_ref[h * 2 * nk + nk:(h + 1) * 2 * nk, c]
                b2 = b2_ref[h * nk:(h + 1) * nk, c]
                w = w + jnp.where((s2 + s1) >= tau_ref[h][:, c], b2, 0.0) * a1
            parts.append(w)
        return parts[0] if len(parts) == 1 else jnp.concatenate(parts, axis=0)

    gates = [gate_of(c) for c in subs]
    h_t = [jnp.dot(u_ref[...], xt_ref[:, c], preferred_element_type=F32) for c in subs]
    acts = [(0.5 * hh * (1.0 + lax.erf(hh * (2.0 ** -0.5))) * g).astype(BF16) for hh, g in zip(h_t, gates)]
    upd = [lax.dot_general(act, v_ref[...], TN, preferred_element_type=F32) for act in acts]
    for c, x in zip(subs, upd):
        o_ref[c, :] += x


def peer_experts(x_t, s_t, a1, b2, tau, u_tab, v_tab, *, heads, nk):
    d, n = x_t.shape
    ne = u_tab.shape[0]
    tm = _tile(n, 512)
    te = 4 * nk
    tok = lambda rows: pl.BlockSpec((rows, tm), lambda i, e: (0, i), pipeline_mode=pl.Buffered(1))
    return pl.pallas_call(
        functools.partial(_peer_expert_kernel, heads=heads, nk=nk, te=te),
        out_shape=jax.ShapeDtypeStruct((n, d), F32),
        grid=(n // tm, ne // te),
        in_specs=[tok(d), tok(heads * 2 * nk), tok(heads * nk), tok(heads * nk),
                  pl.BlockSpec((heads, 1, tm), lambda i, e: (0, 0, i), pipeline_mode=pl.Buffered(1)),
                  pl.BlockSpec((te, d), lambda i, e: (e, 0)), pl.BlockSpec((te, d), lambda i, e: (e, 0))],
        out_specs=pl.BlockSpec((tm, d), lambda i, e: (i, 0)),
        compiler_params=_cparams("parallel", "arbitrary"),
        name="peer_experts",
    )(x_t, s_t, a1, b2, tau, u_tab, v_tab)


def peer_layer(h_parts, ln_g, wq_t, keys, u_tab, v_tab, *, heads, nk):
    x_t = rms_norm(h_parts, ln_g, transpose_out=True)
    s_t = matmul(wq_t, x_t, mode="keys", extra=(keys,), tm=2 * nk, tn=512, nk=nk)
    a1, b2, tau = peer_select(s_t, heads=heads, nk=nk)
    return peer_experts(x_t, s_t, a1, b2, tau, u_tab, v_tab, heads=heads, nk=nk)


def _paged_update(s, v, m_s, l_s, acc_s):
    m_prev = m_s[...]
    m_new = jnp.maximum(m_prev, jnp.max(s, axis=1, keepdims=True))
    alpha = jnp.exp(m_prev - m_new)
    p = jnp.exp(s - _lane_tile(m_new, s.shape[1]))
    l_s[...] = alpha * l_s[...] + jnp.sum(p, axis=1, keepdims=True)
    acc_s[...] = alpha * acc_s[...] + jnp.dot(p.astype(BF16), v.astype(BF16), preferred_element_type=F32)
    m_s[...] = m_new


def _head_mask(rows, keys, heads, t_new, causal_new):
    r = lax.broadcasted_iota(jnp.int32, (rows, keys), 0)
    c = lax.broadcasted_iota(jnp.int32, (rows, keys), 1)
    ok = (c % heads) == ((r // t_new) % heads)
    if causal_new:
        ok = ok & ((c // heads) <= (r % t_new))
    return jnp.where(ok, 0.0, -jnp.inf)


def _paged_diff_kernel(pt_ref, lam_ref, q_ref, kc_ref, vc_ref, kn_ref, vn_ref, g_ref, o_ref,
                       m_s, l_s, acc_s, mask_s, *, heads, t_new, post):
    p = pl.program_id(1)
    rows = q_ref.shape[0]

    @pl.when(p == 0)
    def _():
        _flash_init(m_s, l_s, acc_s)
        mask_s[...] = _head_mask(rows, mask_s.shape[1], heads, t_new, False)
        s = lax.dot_general(q_ref[...], kn_ref[...].astype(BF16), NT, preferred_element_type=F32)
        _paged_update(s + _head_mask(rows, kn_ref.shape[0], heads, t_new, True), vn_ref[...], m_s, l_s, acc_s)

    @pl.when(p > 0)
    def _():
        s = lax.dot_general(q_ref[...], kc_ref[...].astype(BF16), NT, preferred_element_type=F32)
        _paged_update(s + mask_s[...], vc_ref[...], m_s, l_s, acc_s)

    @pl.when(p == pl.num_programs(1) - 1)
    def _():
        a = acc_s[...] / l_s[...]
        o = a[:rows // 2] - lam_ref[0] * a[rows // 2:]
        y = o * lax.rsqrt(jnp.mean(o * o, axis=-1, keepdims=True) + EPS) * g_ref[...]
        o_ref[...] = (y * post).astype(o_ref.dtype)


def _new_rows(x, nseq, width):
    x = x.reshape(nseq, -1, width)
    pad = -x.shape[1] % LANES
    return jnp.pad(x, ((0, 0), (0, pad), (0, 0)))


def diff_attention_paged(qa, ka, va, cache_k, cache_v, page_table, lam, subln_g, post, *, heads, t_new):
    nseq, n_pages = page_table.shape
    dv = 2 * A_DH
    rows = 2 * heads * t_new
    q5 = qa.reshape(nseq, t_new, heads, 2, A_DH) * (A_DH ** -0.5)
    zero = jnp.zeros_like(q5[:, :, :, 0])
    qm = jnp.stack([jnp.concatenate([q5[:, :, :, 0], zero], axis=-1),
                    jnp.concatenate([zero, q5[:, :, :, 1]], axis=-1)], axis=1)
    qm = jnp.transpose(qm, (0, 1, 3, 2, 4)).reshape(nseq, rows, dv).astype(BF16)
    kn, vn = _new_rows(ka, nseq, dv), _new_rows(va, nseq, dv)
    page = lambda s, p, pt: (pt[s, jnp.maximum(p - 1, 0)], 0, 0)
    per_seq = lambda s, p, pt: (s, 0, 0)
    keys = PAGE * heads
    out = pl.pallas_call(
        functools.partial(_paged_diff_kernel, heads=heads, t_new=t_new, post=post),
        out_shape=jax.ShapeDtypeStruct((nseq, rows // 2, dv), BF16),
        grid_spec=pltpu.PrefetchScalarGridSpec(
            num_scalar_prefetch=1,
            grid=(nseq, n_pages + 1),
            in_specs=[pl.BlockSpec(memory_space=pltpu.SMEM),
                      pl.BlockSpec((None, rows, dv), per_seq),
                      pl.BlockSpec((None, keys, dv), page), pl.BlockSpec((None, keys, dv), page),
                      pl.BlockSpec((None, kn.shape[1], dv), per_seq), pl.BlockSpec((None, kn.shape[1], dv), per_seq),
                      pl.BlockSpec((1, dv), lambda s, p, pt: (0, 0))],
            out_specs=pl.BlockSpec((None, rows // 2, dv), per_seq),
            scratch_shapes=[pltpu.VMEM((rows, LANES), F32), pltpu.VMEM((rows, LANES), F32), pltpu.VMEM((rows, dv), F32),
                            pltpu.VMEM((rows, keys), F32)]),
        compiler_params=_cparams("parallel", "arbitrary"),
        name="diff_attention_paged",
    )(page_table, lam.reshape(1).astype(F32), qm, cache_k, cache_v, kn, vn, subln_g.reshape(1, dv).astype(F32))
    out = jnp.transpose(out.reshape(nseq, heads, t_new, dv), (0, 2, 1, 3))
    return out.reshape(nseq * t_new, heads * dv)


def _page_suffix_kernel(lf_ref, later_ref, tot_ref):
    a = lax.broadcasted_iota(jnp.int32, (PAGE, PAGE), 0)
    b = lax.broadcasted_iota(jnp.int32, (PAGE, PAGE), 1)
    after = (b > a).astype(F32)
    ones = jnp.ones((PAGE, PAGE), F32)
    for i in range(lf_ref.shape[0]):
        lf = lf_ref[i]
        later_ref[i] = jnp.dot(after, lf, precision=HI, preferred_element_type=F32)
        tot_ref[i] = jnp.dot(ones, lf, precision=HI, preferred_element_type=F32)


def page_suffix_sums(cache_lf):
    n_pool, _, heads = cache_lf.shape
    pb = math.gcd(n_pool, 8)
    blk = pl.BlockSpec((pb, PAGE, heads), lambda i: (i, 0, 0))
    return pl.pallas_call(
        _page_suffix_kernel,
        out_shape=[jax.ShapeDtypeStruct(cache_lf.shape, F32)] * 2,
        grid=(n_pool // pb,),
        in_specs=[blk],
        out_specs=[blk, blk],
        compiler_params=_cparams("parallel"),
        name="page_suffix_sums",
    )(cache_lf)


def _paged_fox_kernel(pt_ref, q_ref, kc_ref, vc_ref, later_ref, tot_ref, kn_ref, vn_ref, lfn_ref, o_ref,
                      m_s, l_s, acc_s, suf_s, mask_s, *, heads, t_new):
    p = pl.program_id(1)
    rows = q_ref.shape[0]

    @pl.when(p == 0)
    def _():
        _flash_init(m_s, l_s, acc_s)
        suf_s[...] = jnp.zeros(suf_s.shape, F32)
        mask_s[...] = _head_mask(rows, mask_s.shape[1], heads, t_new, False)
        nkeys = kn_ref.shape[0]
        a = lax.broadcasted_iota(jnp.int32, (nkeys, nkeys), 0)
        b = lax.broadcasted_iota(jnp.int32, (nkeys, nkeys), 1)
        upto = (((a % heads) == (b % heads)) & ((a // heads) <= (b // heads))).astype(F32)
        ncum = jnp.dot(jnp.broadcast_to(lfn_ref[...], (SUBLANES, nkeys)), upto, precision=HI,
                       preferred_element_type=F32)[0:1, :]
        s = lax.dot_general(q_ref[...], kn_ref[...].astype(BF16), NT, preferred_element_type=F32) - ncum
        _paged_update(s + _head_mask(rows, nkeys, heads, t_new, True), vn_ref[...], m_s, l_s, acc_s)

    @pl.when(p > 0)
    def _():
        s = lax.dot_general(q_ref[...], kc_ref[...].astype(BF16), NT, preferred_element_type=F32)
        s = s + (later_ref[...] + suf_s[...]) + mask_s[...]
        _paged_update(s, vc_ref[...], m_s, l_s, acc_s)
        suf_s[...] = suf_s[...] + tot_ref[...]

    @pl.when(p == pl.num_programs(1) - 1)
    def _():
        o_ref[...] = (acc_s[...] / l_s[...]).astype(o_ref.dtype)


def fox_attention_paged(q, k, v, logf_new, cache_k, cache_v, later, tot, page_table, *, heads, dh, t_new):
    nseq, n_pages = page_table.shape
    rows = heads * t_new
    qm = jnp.transpose((q * (dh ** -0.5)).reshape(nseq, t_new, heads, dh), (0, 2, 1, 3))
    qm = qm.reshape(nseq, rows, dh).astype(BF16)
    kn, vn = _new_rows(k, nseq, dh), _new_rows(v, nseq, dh)
    lfn = _new_rows(logf_new.astype(F32), nseq, 1).reshape(nseq, 1, -1)
    page = lambda s, p, pt: (pt[s, n_pages - jnp.maximum(p, 1)], 0, 0)
    per_seq = lambda s, p, pt: (s, 0, 0)
    keys = PAGE * heads
    nkeys = kn.shape[1]
    out = pl.pallas_call(
        functools.partial(_paged_fox_kernel, heads=heads, t_new=t_new),
        out_shape=jax.ShapeDtypeStruct((nseq, rows, dh), BF16),
        grid_spec=pltpu.PrefetchScalarGridSpec(
            num_scalar_prefetch=1,
            grid=(nseq, n_pages + 1),
            in_specs=[pl.BlockSpec((None, rows, dh), per_seq),
                      pl.BlockSpec((None, keys, dh), page), pl.BlockSpec((None, keys, dh), page),
                      pl.BlockSpec((None, 1, keys), page), pl.BlockSpec((None, 1, keys), page),
                      pl.BlockSpec((None, nkeys, dh), per_seq), pl.BlockSpec((None, nkeys, dh), per_seq),
                      pl.BlockSpec((None, 1, nkeys), per_seq)],
            out_specs=pl.BlockSpec((None, rows, dh), per_seq),
            scratch_shapes=[pltpu.VMEM((rows, LANES), F32), pltpu.VMEM((rows, LANES), F32), pltpu.VMEM((rows, dh), F32),
                            pltpu.VMEM((1, keys), F32), pltpu.VMEM((rows, keys), F32)]),
        compiler_params=_cparams("parallel", "arbitrary"),
        name="fox_attention_paged",
    )(page_table, qm, cache_k, cache_v, later, tot, kn, vn, lfn)
    out = jnp.transpose(out.reshape(nseq, heads, t_new, dh), (0, 2, 1, 3))
    return out.reshape(nseq * t_new, heads * dh)


def _rope_tables(pos):
    rot = A_DH // 4
    half = rot // 2
    inv = jnp.power(ROPE_THETA, -jnp.arange(0, rot, 2, dtype=F32) / rot)
    ang = pos.astype(F32)[:, None] * inv[None, :]
    cos, sin = jnp.cos(ang), jnp.sin(ang)
    n = pos.shape[0]
    ones = jnp.ones((n, A_DH - rot), F32)
    zeros = jnp.zeros((n, A_DH - rot), F32)
    zh = jnp.zeros((n, half), F32)
    c = jnp.concatenate([cos, cos, ones], axis=1)
    sp = jnp.concatenate([-sin, zh, zeros], axis=1)
    sm = jnp.concatenate([zh, sin, zeros], axis=1)
    rep = LANES // A_DH
    return jnp.tile(c, (1, rep)), jnp.tile(sp, (1, rep)), jnp.tile(sm, (1, rep))


def _pad_rows(x, m):
    return x if x.shape[0] == m else jnp.pad(x, ((0, m - x.shape[0]), (0, 0)))


def _even_layer(h, pos, n_tok, attend, conv_buf, rec0, gdn_shape, w, lam, lam_init):
    heads_b = w["heads_b"]
    xn = rms_norm([h], w["ln_mix"])
    tabs = tuple(_pad_rows(t, h.shape[0]) for t in _rope_tables(pos))
    qa = matmul(xn, w["w_qa"], mode="rope", extra=tabs)
    ka = matmul(xn, w["w_ka"], mode="rope", extra=tabs)
    va = matmul(xn, w["w_va"])
    qkv = matmul(xn, w["w_qkv"])
    z = matmul(xn, w["w_z"])
    ab = matmul(xn, w["w_ab"])
    att = attend(qa[:n_tok], ka[:n_tok], va[:n_tok], lam, w["subln_g"], 1.0 - lam_init)
    bsz, seq = gdn_shape
    seq_p = -(-seq // DN_CHUNK) * DN_CHUNK

    def seq_pad(x):
        x = x[:n_tok].reshape(bsz, seq, x.shape[-1])
        if seq_p != seq:
            x = jnp.pad(x, ((0, 0), (0, seq_p - seq), (0, 0)))
        return x

    gate = lambda x: jnp.transpose(seq_pad(x), (0, 2, 1))[..., None]
    buf8 = jnp.pad(conv_buf.astype(F32), ((0, 0), (SUBLANES - conv_buf.shape[1], 0), (0, 0)))
    dn, rec = gated_deltanet(
        seq_pad(qkv).reshape(bsz * seq_p, -1), seq_pad(z).reshape(bsz * seq_p, -1),
        gate(ab[:, :heads_b]), gate(ab[:, heads_b:2 * heads_b]), buf8, rec0.astype(F32),
        w["conv_w"], w["a_log"], w["dt_bias"], w["dn_norm_g"],
        bsz=bsz, seq=seq_p, heads=heads_b, valid=min(seq, DN_CHUNK))
    dn = dn.reshape(bsz, seq_p, -1)[:, :seq].reshape(n_tok, -1)
    xp = jnp.concatenate([conv_buf.astype(F32), qkv[:n_tok].reshape(bsz, seq, -1)], axis=1)
    conv_new = xp[:, seq:]
    m = h.shape[0]
    h = matmul(_pad_rows(att, m), w["w_out_a"], mode="resid", extra=(h,))
    h = matmul(_pad_rows(dn, m), w["w_out_b"], mode="resid", extra=(h,))
    return h, ka[:n_tok], va[:n_tok], conv_new, rec


def _odd_layer(h_parts, n_tok, attend, w):
    h, xn = rms_norm(h_parts, w["ln_mix"], emit_sum=True)
    q = matmul(xn, w["w_q"])
    k = matmul(xn, w["w_k"])
    v = matmul(xn, w["w_v"])
    logf = matmul(xn, w["w_f"], mode="logsig", extra=(w["b_f"],))[:, :w["heads_c"]]
    ctx = attend(q[:n_tok], k[:n_tok], v[:n_tok], logf[:n_tok])
    h = matmul(_pad_rows(ctx, h.shape[0]), w["w_out"], mode="resid", extra=(h,))
    return h, k[:n_tok], v[:n_tok], logf[:n_tok]


def kernel(x_prompt, x_sample, cache_a_k, cache_a_v, state_b_conv, state_b_rec, cache_c_k, cache_c_v, cache_c_logf, page_table, ln_mix, ln_ffn, ln_out, w_in_even, w_out_even, lam_q1, lam_k1, lam_q2, lam_k2, subln_g, conv_w, a_log, dt_bias, dn_norm_g, w_in_odd, b_forget, w_out_odd, peer_wq, peer_keys, peer_u, peer_v):
    bsz, seq, d = x_prompt.shape
    dbsz, dseq, _ = x_sample.shape
    depth = ln_mix.shape[0]
    heads_a, a_dv = cache_a_k.shape[3], cache_a_k.shape[4]
    a_w = heads_a * a_dv
    heads_b, dk = state_b_rec.shape[2], state_b_rec.shape[3]
    heads_c, c_dh = cache_c_k.shape[3], cache_c_k.shape[4]
    c_w = heads_c * c_dh
    nk, dq = peer_keys.shape[2], peer_keys.shape[3]
    assert nk == dq and a_dv == 2 * A_DH
    peer_heads = peer_wq.shape[-1] // (2 * dq)
    past_len = page_table.shape[1] * PAGE
    n_p, n_s = bsz * seq, dbsz * dseq
    m_s = -(-n_s // LANES) * LANES

    hp = x_prompt.reshape(n_p, d).astype(F32)
    hs = _pad_rows(x_sample.reshape(n_s, d).astype(F32), m_s)
    pos_p = jnp.tile(jnp.arange(seq), bsz)
    pos_s = jnp.tile(past_len + jnp.arange(dseq), dbsz)
    bf = lambda x: x.astype(BF16)
    lane_pad = lambda x: jnp.pad(x, ((0, 0), (0, LANES - x.shape[1])))

    outs_p = {k: [] for k in ("ak", "av", "bc", "br", "ck", "cv", "cf")}
    outs_s = {k: [] for k in ("ak", "av", "bc", "br", "ck", "cv", "cf")}
    peer_p = peer_s = None
    for li in range(depth):
        if li % 2 == 0:
            e = li // 2
            lam_init = 0.8 - 0.6 * math.exp(-0.3 * li)
            lam = (jnp.exp(jnp.sum(lam_q1[e] * lam_k1[e]).astype(F32))
                   - jnp.exp(jnp.sum(lam_q2[e] * lam_k2[e]).astype(F32)) + lam_init)
            wi, wo = w_in_even[e], w_out_even[e]
            o1, o2, o3 = a_w, 2 * a_w, 3 * a_w
            o4 = o3 + 3 * heads_b * dk
            o5 = o4 + heads_b * dk
            w = dict(heads_b=heads_b, ln_mix=ln_mix[li], subln_g=subln_g[e],
                     w_qa=bf(wi[:, :o1]), w_ka=bf(wi[:, o1:o2]), w_va=bf(wi[:, o2:o3]), w_qkv=bf(wi[:, o3:o4]),
                     w_z=bf(wi[:, o4:o5]), w_ab=bf(lane_pad(wi[:, o5:])),
                     w_out_a=bf(wo[:a_w]), w_out_b=bf(wo[a_w:]),
                     conv_w=conv_w[e].astype(F32), a_log=a_log[e], dt_bias=dt_bias[e], dn_norm_g=dn_norm_g[e])
            if peer_p is not None:
                hp, hs = hp + peer_p, hs + peer_s
            attend_p = functools.partial(diff_attention_prompt, bsz=bsz, seq=seq, heads=heads_a)
            hp, ka, va, cb, rs = _even_layer(
                hp, pos_p, n_p, attend_p, jnp.zeros((bsz, CONV_TAPS - 1, 3 * heads_b * dk), F32),
                jnp.zeros((bsz, heads_b, dk, dk), F32), (bsz, seq), w, lam, lam_init)
            for key, val in zip(("ak", "av", "bc", "br"), (ka, va, cb, rs)):
                outs_p[key].append(val)
            pool_k = cache_a_k[e].reshape(-1, PAGE * heads_a, a_dv)
            pool_v = cache_a_v[e].reshape(-1, PAGE * heads_a, a_dv)
            attend_s = lambda q, k, v, lam_, g_, post_: diff_attention_paged(
                q, k, v, pool_k, pool_v, page_table, lam_, g_, post_, heads=heads_a, t_new=dseq)
            hs, ka, va, cb, rs = _even_layer(
                hs, pos_s, n_s, attend_s, state_b_conv[e], state_b_rec[e], (dbsz, dseq), w, lam, lam_init)
            for key, val in zip(("ak", "av", "bc", "br"), (ka, va, cb, rs)):
                outs_s[key].append(val)
            parts_p, parts_s = [hp], [hs]
        else:
            o = li // 2
            wi = w_in_odd[o]
            w = dict(heads_c=heads_c, ln_mix=ln_mix[li], w_q=bf(wi[:, :c_w]), w_k=bf(wi[:, c_w:2 * c_w]),
                     w_v=bf(wi[:, 2 * c_w:3 * c_w]), w_f=bf(lane_pad(wi[:, 3 * c_w:])),
                     b_f=lane_pad(b_forget[o].reshape(1, heads_c).astype(F32)), w_out=bf(w_out_odd[o]))

            def attend_p(q, k, v, logf):
                f = cumsum_time(lane_pad(logf).reshape(bsz, seq, LANES))[:, :, :heads_c]
                f_row = jnp.transpose(f, (0, 2, 1))[:, :, None, :]
                return fox_attention_prompt(q, k, v, f_row, bsz=bsz, seq=seq, heads=heads_c, dh=c_dh)

            later, tot = page_suffix_sums(cache_c_logf[o].astype(F32))
            flat = lambda x: x.reshape(-1, 1, PAGE * heads_c)
            pool_k = cache_c_k[o].reshape(-1, PAGE * heads_c, c_dh)
            pool_v = cache_c_v[o].reshape(-1, PAGE * heads_c, c_dh)
            attend_s = lambda q, k, v, logf: fox_attention_paged(
                q, k, v, logf, pool_k, pool_v, flat(later), flat(tot), page_table,
                heads=heads_c, dh=c_dh, t_new=dseq)
            parts_p = [hp] if peer_p is None else [hp, peer_p]
            parts_s = [hs] if peer_s is None else [hs, peer_s]
            hp, kc, vc, lf = _odd_layer(parts_p, n_p, attend_p, w)
            for key, val in zip(("ck", "cv", "cf"), (kc, vc, lf)):
                outs_p[key].append(val)
            hs, kc, vc, lf = _odd_layer(parts_s, n_s, attend_s, w)
            for key, val in zip(("ck", "cv", "cf"), (kc, vc, lf)):
                outs_s[key].append(val)
            parts_p, parts_s = [hp], [hs]
        wq_t = bf(peer_wq[li].T)
        keys = bf(peer_keys[li])
        u_tab, v_tab = bf(peer_u[li]), bf(peer_v[li])
        peer_p = peer_layer(parts_p, ln_ffn[li], wq_t, keys, u_tab, v_tab, heads=peer_heads, nk=nk)
        peer_s = peer_layer(parts_s, ln_ffn[li], wq_t, keys, u_tab, v_tab, heads=peer_heads, nk=nk)

    y_p = rms_norm([hp, peer_p], ln_out, out_dtype=F32).reshape(bsz, seq, d)
    y_s = rms_norm([hs, peer_s], ln_out, out_dtype=F32)[:n_s].reshape(dbsz, dseq, d)

    def stack(vals, *shape):
        return jnp.stack([v.reshape(*shape) for v in vals])

    def group(outs, b, t):
        return (stack(outs["ak"], b, t, heads_a, a_dv), stack(outs["av"], b, t, heads_a, a_dv),
                stack(outs["bc"], b, CONV_TAPS - 1, 3 * heads_b * dk), stack(outs["br"], b, heads_b, dk, dk),
                stack(outs["ck"], b, t, heads_c, c_dh), stack(outs["cv"], b, t, heads_c, c_dh),
                stack(outs["cf"], b, t, heads_c))

    return (y_p, y_s) + group(outs_p, bsz, seq) + group(outs_s, dbsz, dseq)
```

```python
import functools
import math

import jax
import jax.numpy as jnp
from jax import lax
from jax.experimental import pallas as pl
from jax.experimental.pallas import tpu as pltpu

F32 = jnp.float32
BF16 = jnp.bfloat16
HI = lax.Precision.HIGHEST

EPS = 1e-6
A_DH = 64
ROPE_THETA = 500000.0
CONV_TAPS = 4
DN_CHUNK = 64
GDN_HEADS_PER_STEP = 16
FOX_HEADS_PER_STEP = 2
PEER_TOPK = 16
PEER_TOKEN_SUB = 256
PAGE = 128
LANES = 128
SUBLANES = 8
VMEM_LIMIT = 56 * 1024 * 1024

NT = (((1,), (1,)), ((), ()))
TN = (((0,), (0,)), ((), ()))


def _cparams(*sem):
    return pltpu.CompilerParams(dimension_semantics=sem, vmem_limit_bytes=VMEM_LIMIT)


def _tile(n, pref):
    if n <= pref:
        return n
    t = pref
    while n % t:
        t //= 2
    return t


def _dot1(a, b, dims=None):
    a, b = a.astype(BF16), b.astype(BF16)
    if dims is None:
        return jnp.dot(a, b, preferred_element_type=F32)
    return lax.dot_general(a, b, dims, preferred_element_type=F32)


def _dot3(a, b, dims=None):
    ah, bh = a.astype(BF16), b.astype(BF16)
    al, bl = (a - ah.astype(F32)).astype(BF16), (b - bh.astype(F32)).astype(BF16)
    return _dot1(ah, bh, dims) + _dot1(ah, bl, dims) + _dot1(al, bh, dims)


def _rms_kernel(*refs, n_in, emit_sum, transpose_out):
    x = refs[0][...]
    for r in refs[1:n_in]:
        x = x + r[...]
    g = refs[n_in][...]
    outs = refs[n_in + 1:]
    y = x * lax.rsqrt(jnp.mean(x * x, axis=-1, keepdims=True) + EPS) * g
    k = 0
    if emit_sum:
        outs[0][...] = x
        k = 1
    if transpose_out:
        outs[k][...] = y.T.astype(outs[k].dtype)
    else:
        outs[k][...] = y.astype(outs[k].dtype)


def rms_norm(xs, g, *, out_dtype=BF16, emit_sum=False, transpose_out=False):
    m, d = xs[0].shape
    tm = _tile(m, 256)
    row = pl.BlockSpec((tm, d), lambda i: (i, 0))
    out_shape, out_specs = [], []
    if emit_sum:
        out_shape.append(jax.ShapeDtypeStruct((m, d), F32))
        out_specs.append(row)
    if transpose_out:
        out_shape.append(jax.ShapeDtypeStruct((d, m), out_dtype))
        out_specs.append(pl.BlockSpec((d, tm), lambda i: (0, i)))
    else:
        out_shape.append(jax.ShapeDtypeStruct((m, d), out_dtype))
        out_specs.append(row)
    res = pl.pallas_call(
        functools.partial(_rms_kernel, n_in=len(xs), emit_sum=emit_sum, transpose_out=transpose_out),
        out_shape=out_shape,
        grid=(m // tm,),
        in_specs=[row] * len(xs) + [pl.BlockSpec((1, d), lambda i: (0, 0))],
        out_specs=out_specs,
        compiler_params=_cparams("parallel"),
        name="rms_norm",
    )(*xs, g.reshape(1, d).astype(F32))
    return res if emit_sum else res[0]


def _mm_kernel(*refs, mode, nk):
    a_ref, b_ref = refs[0], refs[1]
    o_ref = refs[-1]
    acc = jnp.dot(a_ref[...], b_ref[...], preferred_element_type=F32)
    if mode == "plain":
        o_ref[...] = acc
    elif mode == "resid":
        o_ref[...] = acc + refs[2][...]
    elif mode == "rope":
        c, sp, sm = refs[2][...], refs[3][...], refs[4][...]
        for g in range(acc.shape[1] // LANES):
            x = acc[:, g * LANES:(g + 1) * LANES]
            o_ref[:, g * LANES:(g + 1) * LANES] = (
                x * c + pltpu.roll(x, LANES - 8, axis=1) * sp + pltpu.roll(x, 8, axis=1) * sm)
    elif mode == "logsig":
        x = acc + refs[2][...]
        o_ref[...] = jnp.minimum(x, 0.0) - jnp.log1p(jnp.exp(-jnp.abs(x)))
    elif mode == "keys":
        keys_ref = refs[2]
        for c in range(2):
            q = acc[c * nk:(c + 1) * nk, :].astype(BF16)
            o_ref[c * nk:(c + 1) * nk, :] = jnp.dot(keys_ref[c], q, preferred_element_type=F32)
    else:
        raise ValueError(mode)


def matmul(a, b, *, mode="plain", extra=(), tm=512, tn=512, nk=0):
    m, k = a.shape
    _, n = b.shape
    tm = _tile(m, tm)
    tn = _tile(n, tn)
    in_specs = [pl.BlockSpec((tm, k), lambda i, j: (i, 0)),
                pl.BlockSpec((k, tn), lambda i, j: (0, j))]
    if mode == "resid":
        in_specs.append(pl.BlockSpec((tm, tn), lambda i, j: (i, j)))
    elif mode == "rope":
        in_specs += [pl.BlockSpec((tm, LANES), lambda i, j: (i, 0))] * 3
    elif mode == "logsig":
        in_specs.append(pl.BlockSpec((1, tn), lambda i, j: (0, j)))
    elif mode == "keys":
        in_specs.append(pl.BlockSpec(extra[0].shape, lambda i, j: (0, 0, 0)))
    return pl.pallas_call(
        functools.partial(_mm_kernel, mode=mode, nk=nk),
        out_shape=jax.ShapeDtypeStruct((m, n), F32),
        grid=(m // tm, n // tn),
        in_specs=in_specs,
        out_specs=pl.BlockSpec((tm, tn), lambda i, j: (i, j)),
        compiler_params=_cparams("parallel", "parallel"),
        name="matmul_" + mode,
    )(a, b, *extra)


def _flash_init(m_s, l_s, acc_s):
    m_s[...] = jnp.full(m_s.shape, -jnp.inf, F32)
    l_s[...] = jnp.zeros(l_s.shape, F32)
    acc_s[...] = jnp.zeros(acc_s.shape, F32)


def _lane_tile(x, width):
    return x if width == LANES else jnp.tile(x, (1, width // LANES))


def _flash_update(ss, vs, m_s, l_s, acc_s):
    n = range(len(ss))
    width = ss[0].shape[1]
    m_prev = [m_s[c] for c in n]
    m_new = [jnp.maximum(m_prev[c], jnp.max(ss[c], axis=1, keepdims=True)) for c in n]
    p = [jnp.exp(ss[c] - _lane_tile(m_new[c], width)) for c in n]
    pv = [jnp.dot(p[c].astype(BF16), vs[c], preferred_element_type=F32) for c in n]
    for c in n:
        alpha = jnp.exp(m_prev[c] - m_new[c])
        l_s[c] = alpha * l_s[c] + jnp.sum(p[c], axis=1, keepdims=True)
        acc_s[c] = alpha * acc_s[c] + pv[c]
        m_s[c] = m_new[c]


def _lower_triangle(tq, tk):
    return lax.broadcasted_iota(jnp.int32, (tq, tk), 1) <= lax.broadcasted_iota(jnp.int32, (tq, tk), 0)


def _causal_pairs(n):
    pairs = [(i, j) for i in range(n) for j in range(i + 1)]
    return (jnp.array([p[0] for p in pairs], jnp.int32), jnp.array([p[1] for p in pairs], jnp.int32))


def _diff_flash_kernel(qi_ref, kj_ref, lam_ref, q_ref, k_ref, v_ref, g_ref, o_ref, m_s, l_s, acc_s, *, tq, tk, post):
    i, j = qi_ref[pl.program_id(2)], kj_ref[pl.program_id(2)]

    @pl.when(j == 0)
    def _():
        _flash_init(m_s, l_s, acc_s)

    def step(masked):
        q = (q_ref[...] * (A_DH ** -0.5)).astype(BF16)
        k = k_ref[...].astype(BF16)
        v = v_ref[...].astype(BF16)
        ss = [lax.dot_general(q[:, c * A_DH:(c + 1) * A_DH], k[:, c * A_DH:(c + 1) * A_DH], NT,
                              preferred_element_type=F32) for c in range(2)]
        if masked:
            ss = [jnp.where(_lower_triangle(tq, tk), s, -jnp.inf) for s in ss]
        _flash_update(ss, [v, v], m_s, l_s, acc_s)

    pl.when(j < i)(functools.partial(step, False))
    pl.when(j == i)(functools.partial(step, True))

    @pl.when(j == i)
    def _():
        o = acc_s[0] / l_s[0] - lam_ref[0] * (acc_s[1] / l_s[1])
        y = o * lax.rsqrt(jnp.mean(o * o, axis=-1, keepdims=True) + EPS) * g_ref[...]
        o_ref[...] = (y * post).astype(o_ref.dtype)


def diff_attention_prompt(qa, ka, va, lam, subln_g, post, *, bsz, seq, heads):
    dv = 2 * A_DH
    tq = tk = _tile(seq, 512)
    nq = seq // tq
    qi, kj = _causal_pairs(nq)
    q_map = lambda b, h, p, qi_r, kj_r: (b * nq + qi_r[p], h)
    kv_map = lambda b, h, p, qi_r, kj_r: (b * nq + kj_r[p], h)
    return pl.pallas_call(
        functools.partial(_diff_flash_kernel, tq=tq, tk=tk, post=post),
        out_shape=jax.ShapeDtypeStruct((bsz * seq, heads * dv), BF16),
        grid_spec=pltpu.PrefetchScalarGridSpec(
            num_scalar_prefetch=2,
            grid=(bsz, heads, qi.shape[0]),
            in_specs=[pl.BlockSpec(memory_space=pltpu.SMEM),
                      pl.BlockSpec((tq, dv), q_map),
                      pl.BlockSpec((tk, dv), kv_map),
                      pl.BlockSpec((tk, dv), kv_map),
                      pl.BlockSpec((1, dv), lambda b, h, p, qi_r, kj_r: (0, 0))],
            out_specs=pl.BlockSpec((tq, dv), q_map),
            scratch_shapes=[pltpu.VMEM((2, tq, LANES), F32), pltpu.VMEM((2, tq, LANES), F32),
                            pltpu.VMEM((2, tq, dv), F32)]),
        compiler_params=_cparams("parallel", "parallel", "arbitrary"),
        name="diff_attention_prompt",
    )(qi, kj, lam.reshape(1).astype(F32), qa, ka, va, subln_g.reshape(1, dv).astype(F32))


def _fox_flash_kernel(qi_ref, kj_ref, q_ref, k_ref, v_ref, fk_ref, o_ref, m_s, l_s, acc_s, *, tq, tk, dh, hb, scale):
    i, j = qi_ref[pl.program_id(2)], kj_ref[pl.program_id(2)]

    @pl.when(j == 0)
    def _():
        _flash_init(m_s, l_s, acc_s)

    def step(masked):
        q = (q_ref[...] * scale).astype(BF16)
        k = k_ref[...].astype(BF16)
        v = v_ref[...].astype(BF16)
        cols = [slice(h * dh, (h + 1) * dh) for h in range(hb)]
        ss = [lax.dot_general(q[:, c], k[:, c], NT, preferred_element_type=F32) - fk_ref[h]
              for h, c in enumerate(cols)]
        if masked:
            ss = [jnp.where(_lower_triangle(tq, tk), s, -jnp.inf) for s in ss]
        _flash_update(ss, [v[:, c] for c in cols], m_s, l_s, acc_s)

    pl.when(j < i)(functools.partial(step, False))
    pl.when(j == i)(functools.partial(step, True))

    @pl.when(j == i)
    def _():
        for h in range(hb):
            o_ref[:, h * dh:(h + 1) * dh] = (acc_s[h] / l_s[h]).astype(o_ref.dtype)


def fox_attention_prompt(q, k, v, f_row, *, bsz, seq, heads, dh):
    tq = tk = _tile(seq, 512)
    nq = seq // tq
    hb = math.gcd(heads, FOX_HEADS_PER_STEP)
    w = hb * dh
    qi, kj = _causal_pairs(nq)
    q_map = lambda b, g, p, qi_r, kj_r: (b * nq + qi_r[p], g)
    kv_map = lambda b, g, p, qi_r, kj_r: (b * nq + kj_r[p], g)
    return pl.pallas_call(
        functools.partial(_fox_flash_kernel, tq=tq, tk=tk, dh=dh, hb=hb, scale=dh ** -0.5),
        out_shape=jax.ShapeDtypeStruct((bsz * seq, heads * dh), BF16),
        grid_spec=pltpu.PrefetchScalarGridSpec(
            num_scalar_prefetch=2,
            grid=(bsz, heads // hb, qi.shape[0]),
            in_specs=[pl.BlockSpec((tq, w), q_map),
                      pl.BlockSpec((tk, w), kv_map),
                      pl.BlockSpec((tk, w), kv_map),
                      pl.BlockSpec((None, hb, 1, tk), lambda b, g, p, qi_r, kj_r: (b, g, 0, kj_r[p]))],
            out_specs=pl.BlockSpec((tq, w), q_map),
            scratch_shapes=[pltpu.VMEM((hb, tq, LANES), F32), pltpu.VMEM((hb, tq, LANES), F32),
                            pltpu.VMEM((hb, tq, dh), F32)]),
        compiler_params=_cparams("parallel", "parallel", "arbitrary"),
        name="fox_attention_prompt",
    )(qi, kj, q, k, v, f_row)


def _cumsum_kernel(x_ref, o_ref, carry_s, *, tc):
    @pl.when(pl.program_id(1) == 0)
    def _():
        carry_s[...] = jnp.zeros(carry_s.shape, F32)

    r = lax.broadcasted_iota(jnp.int32, (tc, tc), 0)
    c = lax.broadcasted_iota(jnp.int32, (tc, tc), 1)
    tri = (c <= r).astype(F32)
    y = jnp.dot(tri, x_ref[...], precision=HI, preferred_element_type=F32) + carry_s[...]
    o_ref[...] = y
    carry_s[...] = y[tc - 1:tc, :]


def cumsum_time(x):
    bsz, seq, w = x.shape
    tc = _tile(seq, 256)
    return pl.pallas_call(
        functools.partial(_cumsum_kernel, tc=tc),
        out_shape=jax.ShapeDtypeStruct(x.shape, F32),
        grid=(bsz, seq // tc),
        in_specs=[pl.BlockSpec((None, tc, w), lambda b, i: (b, i, 0))],
        out_specs=pl.BlockSpec((None, tc, w), lambda b, i: (b, i, 0)),
        scratch_shapes=[pltpu.VMEM((1, w), F32)],
        compiler_params=_cparams("parallel", "arbitrary"),
        name="cumsum_time",
    )(x)


def _each(f, *lists):
    return [f(*args) for args in zip(*lists)]


def _unit_lower_inverse(ns, size):
    ri = lax.broadcasted_iota(jnp.int32, (size, size), 0)
    ci = lax.broadcasted_iota(jnp.int32, (size, size), 1)
    eye = (ri == ci).astype(F32)
    bs = min(size, 16)
    nd = [jnp.where((ri // bs) == (ci // bs), n, 0.0) for n in ns]
    x, pw, p = [eye - m for m in nd], _each(_dot3, nd, nd), 2
    while p < bs:
        x = _each(lambda a, b: _dot3(a, eye + b), x, pw)
        if 2 * p < bs:
            pw = _each(_dot3, pw, pw)
        p *= 2
    if size == bs:
        return x
    nb = size // bs
    blk = _each(lambda a, n, m: _dot3(a, n - m), x, ns, nd)
    y = [eye - m for m in blk]
    if nb > 2:
        pw, p = _each(_dot3, blk, blk), 2
        while p < nb:
            y = _each(lambda a, b: _dot3(a, eye + b), y, pw)
            if 2 * p < nb:
                pw = _each(_dot3, pw, pw)
            p *= 2
    return _each(_dot3, y, x)


def _gdn_kernel(nea_ref, dtb_ref, xq_ref, xk_ref, xv_ref, z_ref, a_ref, b_ref,
                cq_ref, ck_ref, cv_ref, wq_ref, wk_ref, wv_ref, s0_ref, ng_ref,
                o_ref, sfin_ref, prev_s, state_s, *, chunk, valid, dk, hb):
    head0, c = pl.program_id(1) * hb, pl.program_id(2)
    halo = SUBLANES

    @pl.when(c == 0)
    def _():
        prev_s[0] = cq_ref[...]
        prev_s[1] = ck_ref[...]
        prev_s[2] = cv_ref[...]
        state_s[...] = s0_ref[...]

    def conv_act(idx, x_ref, w_ref):
        x = x_ref[...]
        w = w_ref[...]
        xx = jnp.concatenate([prev_s[idx], x], axis=0)
        y = x * w[CONV_TAPS - 1:CONV_TAPS, :]
        for s in range(1, CONV_TAPS):
            y = y + pltpu.roll(xx, s, axis=0)[halo:halo + chunk] * w[CONV_TAPS - 1 - s:CONV_TAPS - s, :]
        prev_s[idx] = x[chunk - halo:chunk]
        return y / (1.0 + jnp.exp(-y))

    aq_all, ak_all, v_all = conv_act(0, xq_ref, wq_ref), conv_act(1, xk_ref, wk_ref), conv_act(2, xv_ref, wv_ref)
    zz = z_ref[...]
    gate_out = zz / (1.0 + jnp.exp(-zz))
    ri = lax.broadcasted_iota(jnp.int32, (chunk, chunk), 0)
    ci = lax.broadcasted_iota(jnp.int32, (chunk, chunk), 1)
    incl = ci <= ri
    row_b = lax.broadcasted_iota(jnp.int32, (chunk, dk), 0)

    def prepare(i):
        sl = slice(i * dk, (i + 1) * dk)
        aq, ak, v = aq_all[:, sl], ak_all[:, sl], v_all[:, sl]
        q = aq * lax.rsqrt(jnp.sum(aq * aq, axis=-1, keepdims=True) + EPS) * (dk ** -0.5)
        k = ak * lax.rsqrt(jnp.sum(ak * ak, axis=-1, keepdims=True) + EPS)
        x = a_ref[i] + dtb_ref[head0 + i]
        g = nea_ref[head0 + i] * (jnp.maximum(x, 0.0) + jnp.log1p(jnp.exp(-jnp.abs(x))))
        beta = 1.0 / (1.0 + jnp.exp(-b_ref[i]))
        if valid < chunk:
            live = lax.broadcasted_iota(jnp.int32, (chunk, 1), 0) < valid
            g = jnp.where(live, g, 0.0)
            beta = jnp.where(live, beta, 0.0)
            k = jnp.where(live, k, 0.0)
        gc_b = jnp.broadcast_to(g, (chunk, dk))
        s = 1
        while s < chunk:
            gc_b = gc_b + jnp.where(row_b >= s, pltpu.roll(gc_b, s, axis=0), 0.0)
            s *= 2
        gc_row = jnp.concatenate([gc_b] * (LANES // chunk), axis=0).T[:chunk, :chunk]
        decay = jnp.exp(jnp.where(incl, gc_b[:, :chunk] - gc_row, -jnp.inf))
        return q, k, v * beta, k * beta, gc_b, decay

    heads = range(hb)
    q, k, vb, kb, gc_b, decay = zip(*[prepare(i) for i in heads])
    lower = _each(lambda a, b, d: jnp.where(ci < ri, _dot3(a, b, NT) * d, 0.0), kb, k, decay)
    intra = _each(lambda a, b, d: jnp.where(incl, _dot1(a, b, NT) * d, 0.0), q, k, decay)
    inv = _unit_lower_inverse(lower, chunk)
    egc = [jnp.exp(x) for x in gc_b]
    uw = _each(lambda t, a, b, e: _dot3(t, jnp.concatenate([a, b * e], axis=1)), inv, vb, kb, egc)
    state = [state_s[i] for i in heads]
    new_v = _each(lambda x, s: x[:, :dk] - _dot3(x[:, dk:], s), uw, state)
    o = _each(lambda a, e, s, t, nv: _dot1(a * e, s) + _dot1(t, nv), q, egc, state, intra, new_v)
    g_last = [x[chunk - 1:chunk, :] for x in gc_b]
    state = _each(lambda s, gl, a, gc, nv: s * jnp.exp(gl) + _dot3(a * jnp.exp(gl - gc), nv, TN),
                  state, g_last, k, gc_b, new_v)
    for i in heads:
        sl = slice(i * dk, (i + 1) * dk)
        state_s[i] = state[i]
        y = o[i] * lax.rsqrt(jnp.mean(o[i] * o[i], axis=-1, keepdims=True) + EPS) * ng_ref[...]
        o_ref[:, sl] = (y * gate_out[:, sl]).astype(o_ref.dtype)

    @pl.when(c == pl.num_programs(2) - 1)
    def _():
        sfin_ref[...] = state_s[...]


def gated_deltanet(qkv, z, a, b, conv_buf8, s0, conv_w, a_log, dt_bias, norm_g, *, bsz, seq, heads, valid):
    dk = LANES
    chunk = DN_CHUNK
    assert LANES % chunk == 0
    nch = seq // chunk
    assert nch == 1 or valid == chunk
    hb = math.gcd(heads, GDN_HEADS_PER_STEP)
    ng = heads // hb
    w = hb * dk
    col = lambda off: (lambda bb, g, c: (bb * nch + c, off + g))
    buf = lambda off: (lambda bb, g, c: (bb, 0, off + g))
    wsp = lambda off: (lambda bb, g, c: (0, off + g))
    gate = pl.BlockSpec((None, hb, chunk, 1), lambda bb, g, c: (bb, g, c, 0))
    st = pl.BlockSpec((None, hb, dk, dk), lambda bb, g, c: (bb, g, 0, 0))
    smem = pl.BlockSpec(memory_space=pltpu.SMEM)
    return pl.pallas_call(
        functools.partial(_gdn_kernel, chunk=chunk, valid=valid, dk=dk, hb=hb),
        out_shape=[jax.ShapeDtypeStruct((bsz * seq, heads * dk), BF16),
                   jax.ShapeDtypeStruct((bsz, heads, dk, dk), F32)],
        grid=(bsz, ng, nch),
        in_specs=[smem, smem,
                  pl.BlockSpec((chunk, w), col(0)), pl.BlockSpec((chunk, w), col(ng)),
                  pl.BlockSpec((chunk, w), col(2 * ng)), pl.BlockSpec((chunk, w), col(0)),
                  gate, gate,
                  pl.BlockSpec((None, SUBLANES, w), buf(0)), pl.BlockSpec((None, SUBLANES, w), buf(ng)),
                  pl.BlockSpec((None, SUBLANES, w), buf(2 * ng)),
                  pl.BlockSpec((CONV_TAPS, w), wsp(0)), pl.BlockSpec((CONV_TAPS, w), wsp(ng)),
                  pl.BlockSpec((CONV_TAPS, w), wsp(2 * ng)),
                  st, pl.BlockSpec((1, dk), lambda bb, g, c: (0, 0))],
        out_specs=[pl.BlockSpec((chunk, w), col(0)), st],
        scratch_shapes=[pltpu.VMEM((3, SUBLANES, w), F32), pltpu.VMEM((hb, dk, dk), F32)],
        compiler_params=_cparams("parallel", "parallel", "arbitrary"),
        name="gated_deltanet",
    )(-jnp.exp(a_log.astype(F32)), dt_bias.astype(F32), qkv, qkv, qkv, z, a, b,
      conv_buf8, conv_buf8, conv_buf8, conv_w, conv_w, conv_w, s0, norm_g.reshape(1, dk).astype(F32))


def _extract_top(s, count):
    rows = lax.broadcasted_iota(jnp.int32, s.shape, 0)
    work = s
    vals = []
    for _ in range(count):
        m = jnp.max(work, axis=0, keepdims=True)
        idx = jnp.min(jnp.where(work == m, rows, s.shape[0]), axis=0, keepdims=True)
        work = jnp.where(rows == idx, -jnp.inf, work)
        vals.append(m)
    return vals, work < s


def _peer_select_kernel(s_ref, a1_ref, b2_ref, tau_ref, *, nk):
    s1 = s_ref[0:nk, :]
    s2 = s_ref[nk:2 * nk, :]
    tl = s1.shape[1]
    v1, mem1 = _extract_top(s1, PEER_TOPK)
    v2, mem2 = _extract_top(s2, PEER_TOPK)
    half = PEER_TOPK // 2
    sub = lax.broadcasted_iota(jnp.int32, (PEER_TOPK, tl), 0)
    v2_all = jnp.zeros((PEER_TOPK, tl), F32)
    v1_hi = jnp.zeros((half, tl), F32)
    for t in range(PEER_TOPK):
        v2_all = jnp.where(sub == t, v2[t], v2_all)
    sub_hi = lax.broadcasted_iota(jnp.int32, (half, tl), 0)
    for t in range(half):
        v1_hi = jnp.where(sub_hi == t, v1[half + t], v1_hi)
    cand = jnp.concatenate([v1[t] + v2_all for t in range(half)] + [v1_hi + v2[0]], axis=0)
    top, _ = _extract_top(cand, PEER_TOPK)
    z = jnp.ones((1, tl), F32)
    for t in range(1, PEER_TOPK):
        z = z + jnp.exp(top[t] - top[0])
    a1_ref[...] = jnp.where(mem1, jnp.exp(s1 - v1[0]), 0.0) / z
    b2_ref[...] = jnp.where(mem2, jnp.exp(s2 - v2[0]), 0.0)
    tau_ref[...] = top[PEER_TOPK - 1]


def peer_select(s_t, *, heads, nk):
    n = s_t.shape[1]
    tl = _tile(n, 256)
    fac = pl.BlockSpec((nk, tl), lambda i, h: (h, i))
    return pl.pallas_call(
        functools.partial(_peer_select_kernel, nk=nk),
        out_shape=[jax.ShapeDtypeStruct((heads * nk, n), F32), jax.ShapeDtypeStruct((heads * nk, n), F32),
                   jax.ShapeDtypeStruct((heads, 1, n), F32)],
        grid=(n // tl, heads),
        in_specs=[pl.BlockSpec((2 * nk, tl), lambda i, h: (h, i))],
        out_specs=[fac, fac, pl.BlockSpec((None, 1, tl), lambda i, h: (h, 0, i))],
        compiler_params=_cparams("parallel", "parallel"),
        name="peer_select",
    )(s_t)


def _peer_expert_kernel(xt_ref, s_ref, a1_ref, b2_ref, tau_ref, u_ref, v_ref, o_ref, *, heads, nk, te):
    e = pl.program_id(1)

    @pl.when(e == 0)
    def _():
        o_ref[...] = jnp.zeros(o_ref.shape, F32)

    tm = xt_ref.shape[1]
    tsub = PEER_TOKEN_SUB if tm % PEER_TOKEN_SUB == 0 else tm
    subs = [slice(t * tsub, (t + 1) * tsub) for t in range(tm // tsub)]
    def gate_of(c):
        parts = []
        for a in range(te // nk):
            i1 = e * (te // nk) + a
            w = jnp.zeros((nk, tsub), F32)
            for h in range(heads):
                s1 = s_ref[pl.ds(h * 2 * nk + i1, 1), c]
                a1 = a1_ref[pl.ds(h * nk + i1, 1), c]
                s2 = s_ref[h * 2 * nk + nk:(h + 1) * 2 * nk, c]
                b2 = b2_ref[h * nk:(h + 1) * nk, c]
                w = w + jnp.where((s2 + s1) >= tau_ref[h][:, c], b2, 0.0) * a1
            parts.append(w)
        return parts[0] if len(parts) == 1 else jnp.concatenate(parts, axis=0)

    gates = [gate_of(c) for c in subs]
    h_t = [jnp.dot(u_ref[...], xt_ref[:, c], preferred_element_type=F32) for c in subs]
    acts = [(0.5 * hh * (1.0 + lax.erf(hh * (2.0 ** -0.5))) * g).astype(BF16) for hh, g in zip(h_t, gates)]
    upd = [lax.dot_general(act, v_ref[...], TN, preferred_element_type=F32) for act in acts]
    for c, x in zip(subs, upd):
        o_ref[c, :] += x


def peer_experts(x_t, s_t, a1, b2, tau, u_tab, v_tab, *, heads, nk):
    d, n = x_t.shape
    ne = u_tab.shape[0]
    tm = _tile(n, 512)
    te = 4 * nk
    tok = lambda rows: pl.BlockSpec((rows, tm), lambda i, e: (0, i), pipeline_mode=pl.Buffered(1))
    return pl.pallas_call(
        functools.partial(_peer_expert_kernel, heads=heads, nk=nk, te=te),
        out_shape=jax.ShapeDtypeStruct((n, d), F32),
        grid=(n // tm, ne // te),
        in_specs=[tok(d), tok(heads * 2 * nk), tok(heads * nk), tok(heads * nk),
                  pl.BlockSpec((heads, 1, tm), lambda i, e: (0, 0, i), pipeline_mode=pl.Buffered(1)),
                  pl.BlockSpec((te, d), lambda i, e: (e, 0)), pl.BlockSpec((te, d), lambda i, e: (e, 0))],
        out_specs=pl.BlockSpec((tm, d), lambda i, e: (i, 0)),
        compiler_params=_cparams("parallel", "arbitrary"),
        name="peer_experts",
    )(x_t, s_t, a1, b2, tau, u_tab, v_tab)


def peer_layer(h_parts, ln_g, wq_t, keys, u_tab, v_tab, *, heads, nk):
    x_t = rms_norm(h_parts, ln_g, transpose_out=True)
    s_t = matmul(wq_t, x_t, mode="keys", extra=(keys,), tm=2 * nk, tn=512, nk=nk)
    a1, b2, tau = peer_select(s_t, heads=heads, nk=nk)
    return peer_experts(x_t, s_t, a1, b2, tau, u_tab, v_tab, heads=heads, nk=nk)


def _paged_update(s, v, m_s, l_s, acc_s):
    m_prev = m_s[...]
    m_new = jnp.maximum(m_prev, jnp.max(s, axis=1, keepdims=True))
    alpha = jnp.exp(m_prev - m_new)
    p = jnp.exp(s - _lane_tile(m_new, s.shape[1]))
    l_s[...] = alpha * l_s[...] + jnp.sum(p, axis=1, keepdims=True)
    acc_s[...] = alpha * acc_s[...] + jnp.dot(p.astype(BF16), v.astype(BF16), preferred_element_type=F32)
    m_s[...] = m_new


def _head_mask(rows, keys, heads, t_new, causal_new):
    r = lax.broadcasted_iota(jnp.int32, (rows, keys), 0)
    c = lax.broadcasted_iota(jnp.int32, (rows, keys), 1)
    ok = (c % heads) == ((r // t_new) % heads)
    if causal_new:
        ok = ok & ((c // heads) <= (r % t_new))
    return jnp.where(ok, 0.0, -jnp.inf)


def _paged_diff_kernel(pt_ref, lam_ref, q_ref, kc_ref, vc_ref, kn_ref, vn_ref, g_ref, o_ref,
                       m_s, l_s, acc_s, mask_s, *, heads, t_new, post):
    p = pl.program_id(1)
    rows = q_ref.shape[0]

    @pl.when(p == 0)
    def _():
        _flash_init(m_s, l_s, acc_s)
        mask_s[...] = _head_mask(rows, mask_s.shape[1], heads, t_new, False)
        s = lax.dot_general(q_ref[...], kn_ref[...].astype(BF16), NT, preferred_element_type=F32)
        _paged_update(s + _head_mask(rows, kn_ref.shape[0], heads, t_new, True), vn_ref[...], m_s, l_s, acc_s)

    @pl.when(p > 0)
    def _():
        s = lax.dot_general(q_ref[...], kc_ref[...].astype(BF16), NT, preferred_element_type=F32)
        _paged_update(s + mask_s[...], vc_ref[...], m_s, l_s, acc_s)

    @pl.when(p == pl.num_programs(1) - 1)
    def _():
        a = acc_s[...] / l_s[...]
        o = a[:rows // 2] - lam_ref[0] * a[rows // 2:]
        y = o * lax.rsqrt(jnp.mean(o * o, axis=-1, keepdims=True) + EPS) * g_ref[...]
        o_ref[...] = (y * post).astype(o_ref.dtype)


def _new_rows(x, nseq, width):
    x = x.reshape(nseq, -1, width)
    pad = -x.shape[1] % LANES
    return jnp.pad(x, ((0, 0), (0, pad), (0, 0)))


def diff_attention_paged(qa, ka, va, cache_k, cache_v, page_table, lam, subln_g, post, *, heads, t_new):
    nseq, n_pages = page_table.shape
    dv = 2 * A_DH
    rows = 2 * heads * t_new
    q5 = qa.reshape(nseq, t_new, heads, 2, A_DH) * (A_DH ** -0.5)
    zero = jnp.zeros_like(q5[:, :, :, 0])
    qm = jnp.stack([jnp.concatenate([q5[:, :, :, 0], zero], axis=-1),
                    jnp.concatenate([zero, q5[:, :, :, 1]], axis=-1)], axis=1)
    qm = jnp.transpose(qm, (0, 1, 3, 2, 4)).reshape(nseq, rows, dv).astype(BF16)
    kn, vn = _new_rows(ka, nseq, dv), _new_rows(va, nseq, dv)
    page = lambda s, p, pt: (pt[s, jnp.maximum(p - 1, 0)], 0, 0)
    per_seq = lambda s, p, pt: (s, 0, 0)
    keys = PAGE * heads
    out = pl.pallas_call(
        functools.partial(_paged_diff_kernel, heads=heads, t_new=t_new, post=post),
        out_shape=jax.ShapeDtypeStruct((nseq, rows // 2, dv), BF16),
        grid_spec=pltpu.PrefetchScalarGridSpec(
            num_scalar_prefetch=1,
            grid=(nseq, n_pages + 1),
            in_specs=[pl.BlockSpec(memory_space=pltpu.SMEM),
                      pl.BlockSpec((None, rows, dv), per_seq),
                      pl.BlockSpec((None, keys, dv), page), pl.BlockSpec((None, keys, dv), page),
                      pl.BlockSpec((None, kn.shape[1], dv), per_seq), pl.BlockSpec((None, kn.shape[1], dv), per_seq),
                      pl.BlockSpec((1, dv), lambda s, p, pt: (0, 0))],
            out_specs=pl.BlockSpec((None, rows // 2, dv), per_seq),
            scratch_shapes=[pltpu.VMEM((rows, LANES), F32), pltpu.VMEM((rows, LANES), F32), pltpu.VMEM((rows, dv), F32),
                            pltpu.VMEM((rows, keys), F32)]),
        compiler_params=_cparams("parallel", "arbitrary"),
        name="diff_attention_paged",
    )(page_table, lam.reshape(1).astype(F32), qm, cache_k, cache_v, kn, vn, subln_g.reshape(1, dv).astype(F32))
    out = jnp.transpose(out.reshape(nseq, heads, t_new, dv), (0, 2, 1, 3))
    return out.reshape(nseq * t_new, heads * dv)


def _page_suffix_kernel(lf_ref, later_ref, tot_ref):
    a = lax.broadcasted_iota(jnp.int32, (PAGE, PAGE), 0)
    b = lax.broadcasted_iota(jnp.int32, (PAGE, PAGE), 1)
    after = (b > a).astype(F32)
    ones = jnp.ones((PAGE, PAGE), F32)
    for i in range(lf_ref.shape[0]):
        lf = lf_ref[i]
        later_ref[i] = jnp.dot(after, lf, precision=HI, preferred_element_type=F32)
        tot_ref[i] = jnp.dot(ones, lf, precision=HI, preferred_element_type=F32)


def page_suffix_sums(cache_lf):
    n_pool, _, heads = cache_lf.shape
    pb = math.gcd(n_pool, 8)
    blk = pl.BlockSpec((pb, PAGE, heads), lambda i: (i, 0, 0))
    return pl.pallas_call(
        _page_suffix_kernel,
        out_shape=[jax.ShapeDtypeStruct(cache_lf.shape, F32)] * 2,
        grid=(n_pool // pb,),
        in_specs=[blk],
        out_specs=[blk, blk],
        compiler_params=_cparams("parallel"),
        name="page_suffix_sums",
    )(cache_lf)


def _paged_fox_kernel(pt_ref, q_ref, kc_ref, vc_ref, later_ref, tot_ref, kn_ref, vn_ref, lfn_ref, o_ref,
                      m_s, l_s, acc_s, suf_s, mask_s, *, heads, t_new):
    p = pl.program_id(1)
    rows = q_ref.shape[0]

    @pl.when(p == 0)
    def _():
        _flash_init(m_s, l_s, acc_s)
        suf_s[...] = jnp.zeros(suf_s.shape, F32)
        mask_s[...] = _head_mask(rows, mask_s.shape[1], heads, t_new, False)
        nkeys = kn_ref.shape[0]
        a = lax.broadcasted_iota(jnp.int32, (nkeys, nkeys), 0)
        b = lax.broadcasted_iota(jnp.int32, (nkeys, nkeys), 1)
        upto = (((a % heads) == (b % heads)) & ((a // heads) <= (b // heads))).astype(F32)
        ncum = jnp.dot(jnp.broadcast_to(lfn_ref[...], (SUBLANES, nkeys)), upto, precision=HI,
                       preferred_element_type=F32)[0:1, :]
        s = lax.dot_general(q_ref[...], kn_ref[...].astype(BF16), NT, preferred_element_type=F32) - ncum
        _paged_update(s + _head_mask(rows, nkeys, heads, t_new, True), vn_ref[...], m_s, l_s, acc_s)

    @pl.when(p > 0)
    def _():
        s = lax.dot_general(q_ref[...], kc_ref[...].astype(BF16), NT, preferred_element_type=F32)
        s = s + (later_ref[...] + suf_s[...]) + mask_s[...]
        _paged_update(s, vc_ref[...], m_s, l_s, acc_s)
        suf_s[...] = suf_s[...] + tot_ref[...]

    @pl.when(p == pl.num_programs(1) - 1)
    def _():
        o_ref[...] = (acc_s[...] / l_s[...]).astype(o_ref.dtype)


def fox_attention_paged(q, k, v, logf_new, cache_k, cache_v, later, tot, page_table, *, heads, dh, t_new):
    nseq, n_pages = page_table.shape
    rows = heads * t_new
    qm = jnp.transpose((q * (dh ** -0.5)).reshape(nseq, t_new, heads, dh), (0, 2, 1, 3))
    qm = qm.reshape(nseq, rows, dh).astype(BF16)
    kn, vn = _new_rows(k, nseq, dh), _new_rows(v, nseq, dh)
    lfn = _new_rows(logf_new.astype(F32), nseq, 1).reshape(nseq, 1, -1)
    page = lambda s, p, pt: (pt[s, n_pages - jnp.maximum(p, 1)], 0, 0)
    per_seq = lambda s, p, pt: (s, 0, 0)
    keys = PAGE * heads
    nkeys = kn.shape[1]
    out = pl.pallas_call(
        functools.partial(_paged_fox_kernel, heads=heads, t_new=t_new),
        out_shape=jax.ShapeDtypeStruct((nseq, rows, dh), BF16),
        grid_spec=pltpu.PrefetchScalarGridSpec(
            num_scalar_prefetch=1,
            grid=(nseq, n_pages + 1),
            in_specs=[pl.BlockSpec((None, rows, dh), per_seq),
                      pl.BlockSpec((None, keys, dh), page), pl.BlockSpec((None, keys, dh), page),
                      pl.BlockSpec((None, 1, keys), page), pl.BlockSpec((None, 1, keys), page),
                      pl.BlockSpec((None, nkeys, dh), per_seq), pl.BlockSpec((None, nkeys, dh), per_seq),
                      pl.BlockSpec((None, 1, nkeys), per_seq)],
            out_specs=pl.BlockSpec((None, rows, dh), per_seq),
            scratch_shapes=[pltpu.VMEM((rows, LANES), F32), pltpu.VMEM((rows, LANES), F32), pltpu.VMEM((rows, dh), F32),
                            pltpu.VMEM((1, keys), F32), pltpu.VMEM((rows, keys), F32)]),
        compiler_params=_cparams("parallel", "arbitrary"),
        name="fox_attention_paged",
    )(page_table, qm, cache_k, cache_v, later, tot, kn, vn, lfn)
    out = jnp.transpose(out.reshape(nseq, heads, t_new, dh), (0, 2, 1, 3))
    return out.reshape(nseq * t_new, heads * dh)


def _rope_tables(pos):
    rot = A_DH // 4
    half = rot // 2
    inv = jnp.power(ROPE_THETA, -jnp.arange(0, rot, 2, dtype=F32) / rot)
    ang = pos.astype(F32)[:, None] * inv[None, :]
    cos, sin = jnp.cos(ang), jnp.sin(ang)
    n = pos.shape[0]
    ones = jnp.ones((n, A_DH - rot), F32)
    zeros = jnp.zeros((n, A_DH - rot), F32)
    zh = jnp.zeros((n, half), F32)
    c = jnp.concatenate([cos, cos, ones], axis=1)
    sp = jnp.concatenate([-sin, zh, zeros], axis=1)
    sm = jnp.concatenate([zh, sin, zeros], axis=1)
    rep = LANES // A_DH
    return jnp.tile(c, (1, rep)), jnp.tile(sp, (1, rep)), jnp.tile(sm, (1, rep))


def _pad_rows(x, m):
    return x if x.shape[0] == m else jnp.pad(x, ((0, m - x.shape[0]), (0, 0)))


def _even_layer(h, pos, n_tok, attend, conv_buf, rec0, gdn_shape, w, lam, lam_init):
    heads_b = w["heads_b"]
    xn = rms_norm([h], w["ln_mix"])
    tabs = tuple(_pad_rows(t, h.shape[0]) for t in _rope_tables(pos))
    qa = matmul(xn, w["w_qa"], mode="rope", extra=tabs)
    ka = matmul(xn, w["w_ka"], mode="rope", extra=tabs)
    va = matmul(xn, w["w_va"])
    qkv = matmul(xn, w["w_qkv"])
    z = matmul(xn, w["w_z"])
    ab = matmul(xn, w["w_ab"])
    att = attend(qa[:n_tok], ka[:n_tok], va[:n_tok], lam, w["subln_g"], 1.0 - lam_init)
    bsz, seq = gdn_shape
    seq_p = -(-seq // DN_CHUNK) * DN_CHUNK

    def seq_pad(x):
        x = x[:n_tok].reshape(bsz, seq, x.shape[-1])
        if seq_p != seq:
            x = jnp.pad(x, ((0, 0), (0, seq_p - seq), (0, 0)))
        return x

    gate = lambda x: jnp.transpose(seq_pad(x), (0, 2, 1))[..., None]
    buf8 = jnp.pad(conv_buf.astype(F32), ((0, 0), (SUBLANES - conv_buf.shape[1], 0), (0, 0)))
    dn, rec = gated_deltanet(
        seq_pad(qkv).reshape(bsz * seq_p, -1), seq_pad(z).reshape(bsz * seq_p, -1),
        gate(ab[:, :heads_b]), gate(ab[:, heads_b:2 * heads_b]), buf8, rec0.astype(F32),
        w["conv_w"], w["a_log"], w["dt_bias"], w["dn_norm_g"],
        bsz=bsz, seq=seq_p, heads=heads_b, valid=min(seq, DN_CHUNK))
    dn = dn.reshape(bsz, seq_p, -1)[:, :seq].reshape(n_tok, -1)
    xp = jnp.concatenate([conv_buf.astype(F32), qkv[:n_tok].reshape(bsz, seq, -1)], axis=1)
    conv_new = xp[:, seq:]
    m = h.shape[0]
    h = matmul(_pad_rows(att, m), w["w_out_a"], mode="resid", extra=(h,))
    h = matmul(_pad_rows(dn, m), w["w_out_b"], mode="resid", extra=(h,))
    return h, ka[:n_tok], va[:n_tok], conv_new, rec


def _odd_layer(h_parts, n_tok, attend, w):
    h, xn = rms_norm(h_parts, w["ln_mix"], emit_sum=True)
    q = matmul(xn, w["w_q"])
    k = matmul(xn, w["w_k"])
    v = matmul(xn, w["w_v"])
    logf = matmul(xn, w["w_f"], mode="logsig", extra=(w["b_f"],))[:, :w["heads_c"]]
    ctx = attend(q[:n_tok], k[:n_tok], v[:n_tok], logf[:n_tok])
    h = matmul(_pad_rows(ctx, h.shape[0]), w["w_out"], mode="resid", extra=(h,))
    return h, k[:n_tok], v[:n_tok], logf[:n_tok]


def kernel(x_prompt, x_sample, cache_a_k, cache_a_v, state_b_conv, state_b_rec, cache_c_k, cache_c_v, cache_c_logf, page_table, ln_mix, ln_ffn, ln_out, w_in_even, w_out_even, lam_q1, lam_k1, lam_q2, lam_k2, subln_g, conv_w, a_log, dt_bias, dn_norm_g, w_in_odd, b_forget, w_out_odd, peer_wq, peer_keys, peer_u, peer_v):
    bsz, seq, d = x_prompt.shape
    dbsz, dseq, _ = x_sample.shape
    depth = ln_mix.shape[0]
    heads_a, a_dv = cache_a_k.shape[3], cache_a_k.shape[4]
    a_w = heads_a * a_dv
    heads_b, dk = state_b_rec.shape[2], state_b_rec.shape[3]
    heads_c, c_dh = cache_c_k.shape[3], cache_c_k.shape[4]
    c_w = heads_c * c_dh
    nk, dq = peer_keys.shape[2], peer_keys.shape[3]
    assert nk == dq and a_dv == 2 * A_DH
    peer_heads = peer_wq.shape[-1] // (2 * dq)
    past_len = page_table.shape[1] * PAGE
    n_p, n_s = bsz * seq, dbsz * dseq
    m_s = -(-n_s // LANES) * LANES

    hp = x_prompt.reshape(n_p, d).astype(F32)
    hs = _pad_rows(x_sample.reshape(n_s, d).astype(F32), m_s)
    pos_p = jnp.tile(jnp.arange(seq), bsz)
    pos_s = jnp.tile(past_len + jnp.arange(dseq), dbsz)
    bf = lambda x: x.astype(BF16)
    lane_pad = lambda x: jnp.pad(x, ((0, 0), (0, LANES - x.shape[1])))

    outs_p = {k: [] for k in ("ak", "av", "bc", "br", "ck", "cv", "cf")}
    outs_s = {k: [] for k in ("ak", "av", "bc", "br", "ck", "cv", "cf")}
    peer_p = peer_s = None
    for li in range(depth):
        if li % 2 == 0:
            e = li // 2
            lam_init = 0.8 - 0.6 * math.exp(-0.3 * li)
            lam = (jnp.exp(jnp.sum(lam_q1[e] * lam_k1[e]).astype(F32))
                   - jnp.exp(jnp.sum(lam_q2[e] * lam_k2[e]).astype(F32)) + lam_init)
            wi, wo = w_in_even[e], w_out_even[e]
            o1, o2, o3 = a_w, 2 * a_w, 3 * a_w
            o4 = o3 + 3 * heads_b * dk
            o5 = o4 + heads_b * dk
            w = dict(heads_b=heads_b, ln_mix=ln_mix[li], subln_g=subln_g[e],
                     w_qa=bf(wi[:, :o1]), w_ka=bf(wi[:, o1:o2]), w_va=bf(wi[:, o2:o3]), w_qkv=bf(wi[:, o3:o4]),
                     w_z=bf(wi[:, o4:o5]), w_ab=bf(lane_pad(wi[:, o5:])),
                     w_out_a=bf(wo[:a_w]), w_out_b=bf(wo[a_w:]),
                     conv_w=conv_w[e].astype(F32), a_log=a_log[e], dt_bias=dt_bias[e], dn_norm_g=dn_norm_g[e])
            if peer_p is not None:
                hp, hs = hp + peer_p, hs + peer_s
            attend_p = functools.partial(diff_attention_prompt, bsz=bsz, seq=seq, heads=heads_a)
            hp, ka, va, cb, rs = _even_layer(
                hp, pos_p, n_p, attend_p, jnp.zeros((bsz, CONV_TAPS - 1, 3 * heads_b * dk), F32),
                jnp.zeros((bsz, heads_b, dk, dk), F32), (bsz, seq), w, lam, lam_init)
            for key, val in zip(("ak", "av", "bc", "br"), (ka, va, cb, rs)):
                outs_p[key].append(val)
            pool_k = cache_a_k[e].reshape(-1, PAGE * heads_a, a_dv)
            pool_v = cache_a_v[e].reshape(-1, PAGE * heads_a, a_dv)
            attend_s = lambda q, k, v, lam_, g_, post_: diff_attention_paged(
                q, k, v, pool_k, pool_v, page_table, lam_, g_, post_, heads=heads_a, t_new=dseq)
            hs, ka, va, cb, rs = _even_layer(
                hs, pos_s, n_s, attend_s, state_b_conv[e], state_b_rec[e], (dbsz, dseq), w, lam, lam_init)
            for key, val in zip(("ak", "av", "bc", "br"), (ka, va, cb, rs)):
                outs_s[key].append(val)
            parts_p, parts_s = [hp], [hs]
        else:
            o = li // 2
            wi = w_in_odd[o]
            w = dict(heads_c=heads_c, ln_mix=ln_mix[li], w_q=bf(wi[:, :c_w]), w_k=bf(wi[:, c_w:2 * c_w]),
                     w_v=bf(wi[:, 2 * c_w:3 * c_w]), w_f=bf(lane_pad(wi[:, 3 * c_w:])),
                     b_f=lane_pad(b_forget[o].reshape(1, heads_c).astype(F32)), w_out=bf(w_out_odd[o]))

            def attend_p(q, k, v, logf):
                f = cumsum_time(lane_pad(logf).reshape(bsz, seq, LANES))[:, :, :heads_c]
                f_row = jnp.transpose(f, (0, 2, 1))[:, :, None, :]
                return fox_attention_prompt(q, k, v, f_row, bsz=bsz, seq=seq, heads=heads_c, dh=c_dh)

            later, tot = page_suffix_sums(cache_c_logf[o].astype(F32))
            flat = lambda x: x.reshape(-1, 1, PAGE * heads_c)
            pool_k = cache_c_k[o].reshape(-1, PAGE * heads_c, c_dh)
            pool_v = cache_c_v[o].reshape(-1, PAGE * heads_c, c_dh)
            attend_s = lambda q, k, v, logf: fox_attention_paged(
                q, k, v, logf, pool_k, pool_v, flat(later), flat(tot), page_table,
                heads=heads_c, dh=c_dh, t_new=dseq)
            parts_p = [hp] if peer_p is None else [hp, peer_p]
            parts_s = [hs] if peer_s is None else [hs, peer_s]
            hp, kc, vc, lf = _odd_layer(parts_p, n_p, attend_p, w)
            for key, val in zip(("ck", "cv", "cf"), (kc, vc, lf)):
                outs_p[key].append(val)
            hs, kc, vc, lf = _odd_layer(parts_s, n_s, attend_s, w)
            for key, val in zip(("ck", "cv", "cf"), (kc, vc, lf)):
                outs_s[key].append(val)
            parts_p, parts_s = [hp], [hs]
        wq_t = bf(peer_wq[li].T)
        keys = bf(peer_keys[li])
        u_tab, v_tab = bf(peer_u[li]), bf(peer_v[li])
        peer_p = peer_layer(parts_p, ln_ffn[li], wq_t, keys, u_tab, v_tab, heads=peer_heads, nk=nk)
        peer_s = peer_layer(parts_s, ln_ffn[li], wq_t, keys, u_tab, v_tab, heads=peer_heads, nk=nk)

    y_p = rms_norm([hp, peer_p], ln_out, out_dtype=F32).reshape(bsz, seq, d)
    y_s = rms_norm([hs, peer_s], ln_out, out_dtype=F32)[:n_s].reshape(dbsz, dseq, d)

    def stack(vals, *shape):
        return jnp.stack([v.reshape(*shape) for v in vals])

    def group(outs, b, t):
        return (stack(outs["ak"], b, t, heads_a, a_dv), stack(outs["av"], b, t, heads_a, a_dv),
                stack(outs["bc"], b, CONV_TAPS - 1, 3 * heads_b * dk), stack(outs["br"], b, heads_b, dk, dk),
                stack(outs["ck"], b, t, heads_c, c_dh), stack(outs["cv"], b, t, heads_c, c_dh),
                stack(outs["cf"], b, t, heads_c))

    return (y_p, y_s) + group(outs_p, bsz, seq) + group(outs_s, dbsz, dseq)
```
